```python
import math
import jax, jax.numpy as jnp
from jax import lax
import numpy as np

D_MODEL = 1024
BATCH = 4
SEQ = 4096
DEPTH = 1

SSM_EXPAND = 2
SSM_D_INNER = SSM_EXPAND * D_MODEL
SSM_HEAD_DIM = 64
SSM_HEADS = SSM_D_INNER // SSM_HEAD_DIM
SSM_GROUPS = 4
SSM_HEADS_PER_GROUP = SSM_HEADS // SSM_GROUPS
SSM_D_STATE = 128
SSM_CONV = 4
SSM_CHUNK = 128
SSM_BC_DIM = SSM_GROUPS * SSM_D_STATE
SSM_CONV_DIM = SSM_D_INNER + 2 * SSM_BC_DIM
SSM_NORM_GROUP = SSM_D_INNER // SSM_GROUPS
SSM_DT_MIN = 0.001
SSM_DT_MAX = 0.1

LRU_WIDTH = 1280
LRU_BLOCKS = 10
LRU_BLOCK = LRU_WIDTH // LRU_BLOCKS
LRU_CONV = 4
LRU_C = 8.0

FFN_HIDDEN = -(-8 * D_MODEL // (3 * 256)) * 256

RMS_EPS = 1e-6

N_GATES = 2 * D_MODEL
IN_PROJ_DIM = N_GATES + SSM_D_INNER + SSM_CONV_DIM + SSM_HEADS + 2 * LRU_WIDTH
IN_SPLITS = (
    N_GATES,
    N_GATES + SSM_D_INNER,
    N_GATES + SSM_D_INNER + SSM_CONV_DIM,
    N_GATES + SSM_D_INNER + SSM_CONV_DIM + SSM_HEADS,
    N_GATES + SSM_D_INNER + SSM_CONV_DIM + SSM_HEADS + LRU_WIDTH,
)

kernel_name = "hybrid_ssd_rglru_gated_block"


def rmsnorm(x, w, eps=RMS_EPS):
    xf = x.astype(jnp.float32)
    y = xf * lax.rsqrt(jnp.mean(xf * xf, axis=-1, keepdims=True) + eps)
    return (y * w.astype(jnp.float32)).astype(x.dtype)


def causal_dwconv(u, w, b):
    k_width = w.shape[0]
    length = u.shape[1]
    up = jnp.pad(u, ((0, 0), (k_width - 1, 0), (0, 0)))
    out = b + up[:, 0:length] * w[0]
    for k in range(1, k_width):
        out = out + up[:, k:k + length] * w[k]
    return out


def segsum_exp(cs):
    t = cs.shape[-1]
    mask = jnp.tril(jnp.ones((t, t), dtype=bool))
    diff = cs[..., :, None] - cs[..., None, :]
    return jnp.exp(jnp.where(mask, diff, -jnp.inf))


def ssd_chunked(xdt, dA, bm, cm):
    b, length, g, r, p = xdt.shape
    n = bm.shape[-1]
    l = SSM_CHUNK
    c = length // l
    xdt = xdt.reshape(b, c, l, g, r, p)
    dA = dA.reshape(b, c, l, g, r)
    bm = bm.reshape(b, c, l, g, n)
    cm = cm.reshape(b, c, l, g, n)

    cs = jnp.cumsum(dA, axis=2)
    lmat = segsum_exp(jnp.moveaxis(cs, 2, -1))
    cb = jnp.einsum('bclgn,bcsgn->bcgls', cm, bm)
    y_diag = jnp.einsum('bcgls,bcgrls,bcsgrp->bclgrp', cb, lmat, xdt)
    decay_states = jnp.exp(cs[:, :, -1:] - cs)
    states = jnp.einsum('bclgn,bclgr,bclgrp->bcgrpn', bm, decay_states, xdt)
    chunk_tot = jnp.moveaxis(cs[:, :, -1], 1, -1)
    chunk_cs = jnp.cumsum(jnp.pad(chunk_tot, ((0, 0), (0, 0), (0, 0), (1, 0))), axis=-1)
    decay_chunk = segsum_exp(chunk_cs)
    states = jnp.concatenate([jnp.zeros_like(states[:, :1]), states], axis=1)
    states_in = jnp.einsum('bgrzc,bcgrpn->bzgrpn', decay_chunk[..., :-1, :], states)
    y_off = jnp.einsum('bclgn,bcgrpn,bclgr->bclgrp', cm, states_in, jnp.exp(cs))
    return (y_diag + y_off).reshape(b, length, g, r, p)


def mamba2_mixer(z, xbc, dt_raw, conv_w, conv_b, dt_bias, a_log, d_skip, norm_w):
    b, length, _ = z.shape
    f32 = jnp.float32
    xbc = jax.nn.silu(causal_dwconv(xbc, conv_w, conv_b))
    xs, bm, cm = jnp.split(xbc, [SSM_D_INNER, SSM_D_INNER + SSM_BC_DIM], axis=-1)
    dt = jax.nn.softplus(dt_raw.astype(f32) + dt_bias.astype(f32))
    a = -jnp.exp(a_log.astype(f32))
    g, r, p, n = SSM_GROUPS, SSM_HEADS_PER_GROUP, SSM_HEAD_DIM, SSM_D_STATE
    xs_h = xs.astype(f32).reshape(b, length, g, r, p)
    dt_h = dt.reshape(b, length, g, r)
    y = ssd_chunked(xs_h * dt_h[..., None], dt_h * a.reshape(g, r),
                    bm.astype(f32).reshape(b, length, g, n),
                    cm.astype(f32).reshape(b, length, g, n))
    y = y + xs_h * d_skip.astype(f32).reshape(g, r, 1)
    y = y.reshape(b, length, SSM_D_INNER) * jax.nn.silu(z.astype(f32))
    yg = y.reshape(b, length, SSM_GROUPS, SSM_NORM_GROUP)
    yg = yg * lax.rsqrt(jnp.mean(yg * yg, axis=-1, keepdims=True) + RMS_EPS)
    y = yg.reshape(b, length, SSM_D_INNER) * norm_w.astype(f32)
    return y.astype(z.dtype)


def rglru_mixer(xl, yl, conv_w, conv_b, w_r, b_r, w_i, b_i, lam):
    b, length, _ = xl.shape
    f32 = jnp.float32
    u = causal_dwconv(xl, conv_w, conv_b)
    ub = u.reshape(b, length, LRU_BLOCKS, LRU_BLOCK)
    r_gate = jax.nn.sigmoid((jnp.einsum('blhi,hij->blhj', ub, w_r).reshape(b, length, LRU_WIDTH) + b_r).astype(f32))
    i_gate = jax.nn.sigmoid((jnp.einsum('blhi,hij->blhj', ub, w_i).reshape(b, length, LRU_WIDTH) + b_i).astype(f32))
    log_a = -LRU_C * r_gate * jax.nn.softplus(-lam.astype(f32))
    a_t = jnp.exp(log_a)
    b_t = jnp.sqrt(-jnp.expm1(2.0 * log_a)) * (i_gate * u.astype(f32))

    def combine(left, right):
        a1, b1 = left
        a2, b2 = right
        return a1 * a2, a2 * b1 + b2

    _, h = lax.associative_scan(combine, (a_t, b_t), axis=1)
    out = h * jax.nn.gelu(yl.astype(f32), approximate=True)
    return out.astype(xl.dtype)


def setup_inputs(seed: int = 0) -> dict:
    key = jax.random.key(seed)
    ks = jax.random.split(key, 32)
    f32 = jnp.float32
    nrm = lambda k, shape, scale: jax.random.normal(k, shape, f32) * scale
    L = DEPTH

    x = jax.random.normal(ks[0], (BATCH, SEQ, D_MODEL), f32)
    norm1_w = 1.0 + nrm(ks[1], (L, D_MODEL), 0.05)
    w_in = nrm(ks[2], (L, D_MODEL, IN_PROJ_DIM), D_MODEL ** -0.5)
    b_branch_gate = nrm(ks[3], (L, N_GATES), 0.1)

    ssm_conv_w = nrm(ks[4], (L, SSM_CONV, SSM_CONV_DIM), SSM_CONV ** -0.5)
    ssm_conv_b = nrm(ks[5], (L, SSM_CONV_DIM), 0.02)
    u = jax.random.uniform(ks[6], (L, SSM_HEADS), f32)
    dt0 = jnp.exp(u * (math.log(SSM_DT_MAX) - math.log(SSM_DT_MIN)) + math.log(SSM_DT_MIN))
    dt0 = jnp.maximum(dt0, 1e-4)
    ssm_dt_bias = dt0 + jnp.log(-jnp.expm1(-dt0))
    ssm_a_log = jnp.log(jax.random.uniform(ks[7], (L, SSM_HEADS), f32, 1.0, 16.0))
    ssm_d = 1.0 + nrm(ks[8], (L, SSM_HEADS), 0.1)
    ssm_norm_w = 1.0 + nrm(ks[9], (L, SSM_D_INNER), 0.05)
    w_out_ssm = nrm(ks[10], (L, SSM_D_INNER, D_MODEL), SSM_D_INNER ** -0.5)

    lru_conv_w = nrm(ks[11], (L, LRU_CONV, LRU_WIDTH), LRU_CONV ** -0.5)
    lru_conv_b = nrm(ks[12], (L, LRU_WIDTH), 0.02)
    lru_w_r = nrm(ks[13], (L, LRU_BLOCKS, LRU_BLOCK, LRU_BLOCK), LRU_BLOCK ** -0.5)
    lru_b_r = nrm(ks[14], (L, LRU_WIDTH), 0.02)
    lru_w_i = nrm(ks[15], (L, LRU_BLOCKS, LRU_BLOCK, LRU_BLOCK), LRU_BLOCK ** -0.5)
    lru_b_i = nrm(ks[16], (L, LRU_WIDTH), 0.02)
    a0 = jax.random.uniform(ks[17], (L, LRU_WIDTH), f32, 0.9, 0.999)
    s0 = a0 ** (1.0 / LRU_C)
    lru_lambda = jnp.log(s0) - jnp.log1p(-s0)
    w_out_lru = nrm(ks[18], (L, LRU_WIDTH, D_MODEL), LRU_WIDTH ** -0.5)

    w_out = nrm(ks[19], (L, D_MODEL, D_MODEL), D_MODEL ** -0.5)
    norm2_w = 1.0 + nrm(ks[20], (L, D_MODEL), 0.05)
    w_ffn_in = nrm(ks[21], (L, D_MODEL, 2 * FFN_HIDDEN), D_MODEL ** -0.5)
    w_ffn_out = nrm(ks[22], (L, FFN_HIDDEN, D_MODEL), FFN_HIDDEN ** -0.5)
    norm_f_w = 1.0 + nrm(ks[23], (D_MODEL,), 0.05)

    return {
        "x": x, "norm1_w": norm1_w, "w_in": w_in, "b_branch_gate": b_branch_gate,
        "ssm_conv_w": ssm_conv_w, "ssm_conv_b": ssm_conv_b, "ssm_dt_bias": ssm_dt_bias,
        "ssm_a_log": ssm_a_log, "ssm_d": ssm_d, "ssm_norm_w": ssm_norm_w, "w_out_ssm": w_out_ssm,
        "lru_conv_w": lru_conv_w, "lru_conv_b": lru_conv_b, "lru_w_r": lru_w_r, "lru_b_r": lru_b_r,
        "lru_w_i": lru_w_i, "lru_b_i": lru_b_i, "lru_lambda": lru_lambda, "w_out_lru": w_out_lru,
        "w_out": w_out, "norm2_w": norm2_w, "w_ffn_in": w_ffn_in, "w_ffn_out": w_ffn_out,
        "norm_f_w": norm_f_w,
    }


def reference(x, norm1_w, w_in, b_branch_gate, ssm_conv_w, ssm_conv_b, ssm_dt_bias, ssm_a_log,
              ssm_d, ssm_norm_w, w_out_ssm, lru_conv_w, lru_conv_b, lru_w_r, lru_b_r, lru_w_i,
              lru_b_i, lru_lambda, w_out_lru, w_out, norm2_w, w_ffn_in, w_ffn_out, norm_f_w):
    h = x
    for l in range(DEPTH):
        hn = rmsnorm(h, norm1_w[l])
        proj = hn @ w_in[l]
        gates, z, xbc, dt_raw, lru_x, lru_y = jnp.split(proj, IN_SPLITS, axis=-1)
        gates = jax.nn.sigmoid(gates + b_branch_gate[l])
        g_ssm, g_lru = jnp.split(gates, 2, axis=-1)
        y_ssm = mamba2_mixer(z, xbc, dt_raw, ssm_conv_w[l], ssm_conv_b[l], ssm_dt_bias[l],
                             ssm_a_log[l], ssm_d[l], ssm_norm_w[l]) @ w_out_ssm[l]
        y_lru = rglru_mixer(lru_x, lru_y, lru_conv_w[l], lru_conv_b[l], lru_w_r[l], lru_b_r[l],
                            lru_w_i[l], lru_b_i[l], lru_lambda[l]) @ w_out_lru[l]
        h = h + (g_ssm * y_ssm + g_lru * y_lru) @ w_out[l]
        hn = rmsnorm(h, norm2_w[l])
        gate, up = jnp.split(hn @ w_ffn_in[l], 2, axis=-1)
        h = h + (jax.nn.silu(gate) * up) @ w_ffn_out[l]
    return rmsnorm(h, norm_f_w)
```

```python
import functools

import jax
import jax.numpy as jnp
from jax import lax
from jax.experimental import pallas as pl
from jax.experimental.pallas import tpu as pltpu

F32 = jnp.float32
BF16 = jnp.bfloat16

SSM_GROUPS = 4
SSM_HEAD_DIM = 64
LRU_C = 8.0
RMS_EPS = 1e-6

LANES = 128
SUBLANES = 8
VMEM_LIMIT_BYTES = 56 * 1024 * 1024

SSD_CHUNK = 128
LRU_CHUNK = 256
ROW_TILE = 256
COL_TILE = 512
CONV_PAD = SUBLANES


def _resident(shape):
    nd = len(shape)
    return pl.BlockSpec(shape, lambda *_: (0,) * nd, pipeline_mode=pl.Buffered(1))


def _softplus(x):
    return jnp.maximum(x, 0.0) + jnp.log1p(jnp.exp(-jnp.abs(x)))


def _silu(x):
    return x * jax.nn.sigmoid(x)


def _rms(x, w):
    ms = jnp.mean(x * x, axis=-1, keepdims=True)
    return x * lax.rsqrt(ms + RMS_EPS) * w


def _split3_bf16(v):
    hi = v.astype(BF16)
    r = v - hi.astype(F32)
    mid = r.astype(BF16)
    lo = (r - mid.astype(F32)).astype(BF16)
    return hi, mid, lo


def _inproj_kernel(x_ref, nw_ref, wg_ref, wz_ref, wxbc_ref, wlx_ref, wly_ref, wdt_ref, bg_ref,
                   g_ref, zs_ref, xbc_ref, lx_ref, gy_ref, dt_ref):
    hn = _rms(x_ref[...], nw_ref[...]).astype(BF16)

    def project(w_ref, out_ref, post):
        n = w_ref.shape[1]
        step = min(COL_TILE, n)
        for lo in range(0, n, step):
            hi = min(lo + step, n)
            acc = jnp.dot(hn, w_ref[:, lo:hi], preferred_element_type=F32)
            out_ref[:, lo:hi] = post(acc, lo, hi).astype(out_ref.dtype)

    project(wg_ref, g_ref, lambda a, lo, hi: jax.nn.sigmoid(a + bg_ref[:, lo:hi]))
    project(wz_ref, zs_ref, lambda a, lo, hi: _silu(a))
    project(wxbc_ref, xbc_ref, lambda a, lo, hi: a)
    project(wlx_ref, lx_ref, lambda a, lo, hi: a)
    project(wly_ref, gy_ref, lambda a, lo, hi: jax.nn.gelu(a, approximate=True))
    project(wdt_ref, dt_ref, lambda a, lo, hi: a)


def _in_proj(x2, nw, wg, wz, wxbc, wlx, wly, wdt, bg):
    t, d = x2.shape
    tm = ROW_TILE
    outs = [(wg.shape[1], BF16), (wz.shape[1], BF16), (wxbc.shape[1], BF16),
            (wlx.shape[1], BF16), (wly.shape[1], BF16), (wdt.shape[1], F32)]
    return pl.pallas_call(
        _inproj_kernel,
        grid=(t // tm,),
        in_specs=[pl.BlockSpec((tm, d), lambda i: (i, 0)), _resident(nw.shape),
                  _resident(wg.shape), _resident(wz.shape), _resident(wxbc.shape),
                  _resident(wlx.shape), _resident(wly.shape), _resident(wdt.shape),
                  _resident(bg.shape)],
        out_specs=[pl.BlockSpec((tm, n), lambda i: (i, 0)) for n, _ in outs],
        out_shape=[jax.ShapeDtypeStruct((t, n), dt) for n, dt in outs],
        compiler_params=pltpu.CompilerParams(dimension_semantics=("arbitrary",),
                                             vmem_limit_bytes=VMEM_LIMIT_BYTES),
        name="in_proj",
    )(x2, nw, wg, wz, wxbc, wlx, wly, wdt, bg)


def _conv_block(ext_ref, cw_ref, cb_ref, cols, rows):
    k_width = cw_ref.shape[0]
    out = cb_ref[:, cols]
    for k in range(k_width):
        start = CONV_PAD - (k_width - 1) + k
        out = out + cw_ref[k:k + 1, cols] * ext_ref[start:start + rows, cols]
    return out


def _ssd_kernel(xbc_ref, zs_ref, dt_ref, cw_ref, cb_ref, dtb_ref, alog_ref, dsk_ref, nw_ref,
                yn_ref, ext_ref, act_ref, st_ref, y_ref, cs_ref, rows_ref):
    lc = SSD_CHUNK
    d_inner = zs_ref.shape[1]
    n_state = (xbc_ref.shape[1] - d_inner) // (2 * SSM_GROUPS)
    heads = d_inner // SSM_HEAD_DIM
    heads_per_group = heads // SSM_GROUPS

    @pl.when(pl.program_id(1) == 0)
    def _():
        ext_ref[0:CONV_PAD, :] = jnp.zeros((CONV_PAD, ext_ref.shape[1]), F32)
        st_ref[...] = jnp.zeros(st_ref.shape, F32)

    ext_ref[CONV_PAD:CONV_PAD + lc, :] = xbc_ref[...].astype(F32)
    for lo in range(0, act_ref.shape[1], 2 * LANES):
        cols = slice(lo, lo + 2 * LANES)
        act_ref[:, cols] = _silu(_conv_block(ext_ref, cw_ref, cb_ref, cols, lc))
    ext_ref[0:CONV_PAD, :] = ext_ref[lc:lc + CONV_PAD, :]

    dt = _softplus(dt_ref[...] + dtb_ref[...])
    d_a = dt * (-jnp.exp(alog_ref[...]))
    ri = lax.broadcasted_iota(jnp.int32, (lc, lc), 0)
    ci = lax.broadcasted_iota(jnp.int32, (lc, lc), 1)
    causal = ri >= ci
    tri = causal.astype(BF16)
    hi, mid, lo_ = _split3_bf16(d_a)
    cs = (jnp.dot(tri, hi, preferred_element_type=F32)
          + jnp.dot(tri, mid, preferred_element_type=F32)
          + jnp.dot(tri, lo_, preferred_element_type=F32))
    cs_ref[...] = cs
    cs_t = cs.T[:heads]
    dt_t = dt.T[:heads]
    rows_ref[0] = cs_t
    rows_ref[1] = dt_t
    rows_ref[2] = dt_t * jnp.exp(cs_t[:, lc - 1:lc] - cs_t)

    lane = lax.broadcasted_iota(jnp.int32, (1, LANES), 1)
    first_half = lane < SSM_HEAD_DIM

    for g in range(SSM_GROUPS):
        b_off = d_inner + g * n_state
        c_off = d_inner + SSM_GROUPS * n_state + g * n_state
        bm = act_ref[:, b_off:b_off + n_state]
        cm = act_ref[:, c_off:c_off + n_state]
        cb = lax.dot_general(cm.astype(BF16), bm.astype(BF16), (((1,), (1,)), ((), ())),
                             preferred_element_type=F32)
        bm_t = bm.T
        for pair in range(heads_per_group // 2):
            j = g * (heads_per_group // 2) + pair
            cols = slice(j * LANES, (j + 1) * LANES)
            xs = act_ref[:, cols]
            s_old = st_ref[j]
            y_pair = xs * dsk_ref[:, cols]
            s_new = None
            decay_row = None
            for k in range(2):
                h = 2 * j + k
                keep = first_half if k == 0 else jnp.logical_not(first_half)
                col = jnp.broadcast_to(cs_ref[:, h:h + 1], (lc, lc))
                row = rows_ref[0, h:h + 1, :]
                seg = jnp.exp(jnp.where(causal, col - row, -jnp.inf))
                g_mat = (cb * seg * rows_ref[1, h:h + 1, :]).astype(BF16)
                c_mat = (cm * jnp.exp(col)).astype(BF16)
                lhs = jnp.concatenate([g_mat, c_mat], axis=1)
                xs_k = jnp.where(keep, xs, 0.0).astype(BF16)
                rhs = jnp.concatenate([xs_k, jnp.where(keep, s_old, 0.0).astype(BF16)], axis=0)
                y_pair = y_pair + jnp.dot(lhs, rhs, preferred_element_type=F32)
                bw = (bm_t * rows_ref[2, h:h + 1, :]).astype(BF16)
                upd = jnp.dot(bw, xs_k, preferred_element_type=F32)
                s_new = upd if s_new is None else s_new + upd
                tot = jnp.exp(col[lc - 1:lc, :])
                decay_row = tot if decay_row is None else jnp.where(first_half, decay_row, tot)
            y_ref[:, cols] = y_pair
            st_ref[j] = s_old * decay_row + s_new

    gw = d_inner // SSM_GROUPS
    for g in range(SSM_GROUPS):
        cols = slice(g * gw, (g + 1) * gw)
        yg = y_ref[:, cols] * zs_ref[:, cols].astype(F32)
        yn_ref[:, cols] = _rms(yg, nw_ref[:, cols]).astype(yn_ref.dtype)


def _ssd(xbc, zs, dt_raw, cw, cb, dtb, alog, dsk, nw, batch, seq):
    lc = SSD_CHUNK
    nc = seq // lc
    conv_dim = xbc.shape[1]
    d_inner = zs.shape[1]
    n_state = (conv_dim - d_inner) // (2 * SSM_GROUPS)
    heads = d_inner // SSM_HEAD_DIM
    row = lambda b, c: (b * nc + c, 0)
    return pl.pallas_call(
        _ssd_kernel,
        grid=(batch, nc),
        in_specs=[pl.BlockSpec((lc, conv_dim), row), pl.BlockSpec((lc, d_inner), row),
                  pl.BlockSpec((lc, LANES), row), _resident(cw.shape), _resident(cb.shape),
                  _resident(dtb.shape), _resident(alog.shape), _resident(dsk.shape),
                  _resident(nw.shape)],
        out_specs=pl.BlockSpec((lc, d_inner), row),
        out_shape=jax.ShapeDtypeStruct((batch * seq, d_inner), BF16),
        scratch_shapes=[pltpu.VMEM((CONV_PAD + lc, conv_dim), F32),
                        pltpu.VMEM((lc, conv_dim), F32),
                        pltpu.VMEM((d_inner // LANES, n_state, LANES), F32),
                        pltpu.VMEM((lc, d_inner), F32),
                        pltpu.VMEM((lc, LANES), F32),
                        pltpu.VMEM((3, heads, lc), F32)],
        compiler_params=pltpu.CompilerParams(dimension_semantics=("arbitrary", "arbitrary"),
                                             vmem_limit_bytes=VMEM_LIMIT_BYTES),
        name="ssd_mixer",
    )(xbc, zs, dt_raw, cw, cb, dtb, alog, dsk, nw)


def _shift_rows(v, s, fill):
    n = v.shape[0]
    if s % SUBLANES == 0:
        return jnp.concatenate([jnp.full((s, v.shape[1]), fill, v.dtype), v[:n - s]], axis=0)
    rolled = pltpu.roll(v, s, axis=0)
    t = lax.broadcasted_iota(jnp.int32, v.shape, 0)
    return jnp.where(t >= s, rolled, fill)


def _lru_kernel(lx_ref, gy_ref, cw_ref, cb_ref, wri_ref, br_ref, bi_ref, lam_ref,
                out_ref, ext_ref, h_ref):
    lc = LRU_CHUNK
    width = lx_ref.shape[1]

    @pl.when(pl.program_id(1) == 0)
    def _():
        ext_ref[0:CONV_PAD, :] = jnp.zeros((CONV_PAD, width), F32)
        h_ref[...] = jnp.zeros(h_ref.shape, F32)

    ext_ref[CONV_PAD:CONV_PAD + lc, :] = lx_ref[...].astype(F32)
    for j in range(width // LANES):
        cols = slice(j * LANES, (j + 1) * LANES)
        u = _conv_block(ext_ref, cw_ref, cb_ref, cols, lc)
        ri = jnp.dot(u.astype(BF16), wri_ref[j], preferred_element_type=F32)
        r_gate = jax.nn.sigmoid(ri[:, :LANES] + br_ref[:, cols])
        i_gate = jax.nn.sigmoid(ri[:, LANES:] + bi_ref[:, cols])
        log_a = (-LRU_C) * r_gate * _softplus(-lam_ref[:, cols])
        a = jnp.exp(log_a)
        b = jnp.sqrt(1.0 - jnp.exp(2.0 * log_a)) * (i_gate * u)
        s = 1
        while s < lc:
            a_prev = _shift_rows(a, s, 1.0)
            b_prev = _shift_rows(b, s, 0.0)
            b = a * b_prev + b
            a = a * a_prev
            s *= 2
        h = a * h_ref[:, cols] + b
        h_ref[:, cols] = h[lc - 1:lc, :]
        out_ref[:, cols] = (h * gy_ref[:, cols].astype(F32)).astype(out_ref.dtype)
    ext_ref[0:CONV_PAD, :] = ext_ref[lc:lc + CONV_PAD, :]


def _lru(lx, gy, cw, cb, wri, br, bi, lam, batch, seq):
    lc = LRU_CHUNK
    nc = seq // lc
    width = lx.shape[1]
    row = lambda b, c: (b * nc + c, 0)
    return pl.pallas_call(
        _lru_kernel,
        grid=(batch, nc),
        in_specs=[pl.BlockSpec((lc, width), row), pl.BlockSpec((lc, width), row),
                  _resident(cw.shape), _resident(cb.shape), _resident(wri.shape),
                  _resident(br.shape), _resident(bi.shape), _resident(lam.shape)],
        out_specs=pl.BlockSpec((lc, width), row),
        out_shape=jax.ShapeDtypeStruct((batch * seq, width), BF16),
        scratch_shapes=[pltpu.VMEM((CONV_PAD + lc, width), F32), pltpu.VMEM((1, width), F32)],
        compiler_params=pltpu.CompilerParams(dimension_semantics=("arbitrary", "arbitrary"),
                                             vmem_limit_bytes=VMEM_LIMIT_BYTES),
        name="lru_mixer",
    )(lx, gy, cw, cb, wri, br, bi, lam)


def _out_ffn_kernel(x_ref, yn_ref, lr_ref, g_ref, wos_ref, wol_ref, wo_ref, n2_ref,
                    wfg_ref, wfu_ref, wfo_ref, nf_ref, out_ref, *, final_norm):
    d = x_ref.shape[1]
    y_ssm = jnp.dot(yn_ref[...], wos_ref[...], preferred_element_type=F32)
    y_lru = jnp.dot(lr_ref[...], wol_ref[...], preferred_element_type=F32)
    merged = g_ref[:, :d].astype(F32) * y_ssm + g_ref[:, d:].astype(F32) * y_lru
    h = x_ref[...] + jnp.dot(merged.astype(BF16), wo_ref[...], preferred_element_type=F32)
    hn = _rms(h, n2_ref[...]).astype(BF16)
    hidden = wfg_ref.shape[1]
    acc = h
    for lo in range(0, hidden, COL_TILE):
        hi = min(lo + COL_TILE, hidden)
        gate = jnp.dot(hn, wfg_ref[:, lo:hi], preferred_element_type=F32)
        up = jnp.dot(hn, wfu_ref[:, lo:hi], preferred_element_type=F32)
        act = (_silu(gate) * up).astype(BF16)
        acc = acc + jnp.dot(act, wfo_ref[lo:hi, :], preferred_element_type=F32)
    if final_norm:
        acc = _rms(acc, nf_ref[...])
    out_ref[...] = acc


def _out_ffn(x2, yn, lr, gates, wos, wol, wo, n2, wfg, wfu, wfo, nf, final_norm):
    t, d = x2.shape
    tm = ROW_TILE
    row = lambda i: (i, 0)
    return pl.pallas_call(
        functools.partial(_out_ffn_kernel, final_norm=final_norm),
        grid=(t // tm,),
        in_specs=[pl.BlockSpec((tm, d), row), pl.BlockSpec((tm, yn.shape[1]), row),
                  pl.BlockSpec((tm, lr.shape[1]), row), pl.BlockSpec((tm, gates.shape[1]), row),
                  _resident(wos.shape), _resident(wol.shape), _resident(wo.shape),
                  _resident(n2.shape), _resident(wfg.shape), _resident(wfu.shape),
                  _resident(wfo.shape), _resident(nf.shape)],
        out_specs=pl.BlockSpec((tm, d), row),
        out_shape=jax.ShapeDtypeStruct((t, d), F32),
        compiler_params=pltpu.CompilerParams(dimension_semantics=("arbitrary",),
                                             vmem_limit_bytes=VMEM_LIMIT_BYTES),
        name="out_ffn",
    )(x2, yn, lr, gates, wos, wol, wo, n2, wfg, wfu, wfo, nf)


def _pad_lanes(a):
    pad = (-a.shape[-1]) % LANES
    return jnp.pad(a, [(0, 0)] * (a.ndim - 1) + [(0, pad)])


def kernel(x, norm1_w, w_in, b_branch_gate, ssm_conv_w, ssm_conv_b, ssm_dt_bias, ssm_a_log, ssm_d, ssm_norm_w, w_out_ssm, lru_conv_w, lru_conv_b, lru_w_r, lru_b_r, lru_w_i, lru_b_i, lru_lambda, w_out_lru, w_out, norm2_w, w_ffn_in, w_ffn_out, norm_f_w):
    batch, seq, d = x.shape
    depth = norm1_w.shape[0]
    d_inner = w_out_ssm.shape[1]
    conv_dim = ssm_conv_w.shape[2]
    heads = ssm_dt_bias.shape[1]
    width = w_out_lru.shape[1]
    hidden = w_ffn_out.shape[1]
    n_gates = b_branch_gate.shape[1]
    assert seq % SSD_CHUNK == 0 and seq % LRU_CHUNK == 0 and (batch * seq) % ROW_TILE == 0
    assert d_inner == heads * SSM_HEAD_DIM and width % LANES == 0 and n_gates == 2 * d

    o_z = n_gates
    o_xbc = o_z + d_inner
    o_dt = o_xbc + conv_dim
    o_lx = o_dt + heads
    o_ly = o_lx + width

    h2 = x.reshape(batch * seq, d)
    for l in range(depth):
        wi = w_in[l].astype(BF16)
        row2 = lambda a: a[l].reshape(1, -1)
        g, zs, xbc, lx, gy, dt_raw = _in_proj(
            h2, row2(norm1_w), wi[:, :o_z], wi[:, o_z:o_xbc], wi[:, o_xbc:o_dt],
            wi[:, o_lx:o_ly], wi[:, o_ly:], _pad_lanes(wi[:, o_dt:o_lx]), row2(b_branch_gate))
        yn = _ssd(xbc, zs, dt_raw, ssm_conv_w[l], row2(ssm_conv_b), _pad_lanes(row2(ssm_dt_bias)),
                  _pad_lanes(row2(ssm_a_log)), jnp.repeat(ssm_d[l], SSM_HEAD_DIM).reshape(1, -1),
                  row2(ssm_norm_w), batch, seq)
        wri = jnp.concatenate([lru_w_r[l], lru_w_i[l]], axis=-1).astype(BF16)
        lr = _lru(lx, gy, lru_conv_w[l], row2(lru_conv_b), wri, row2(lru_b_r), row2(lru_b_i),
                  row2(lru_lambda), batch, seq)
        wf = w_ffn_in[l].astype(BF16)
        h2 = _out_ffn(h2, yn, lr, g, w_out_ssm[l].astype(BF16), w_out_lru[l].astype(BF16),
                      w_out[l].astype(BF16), row2(norm2_w), wf[:, :hidden], wf[:, hidden:],
                      w_ffn_out[l].astype(BF16), norm_f_w.reshape(1, -1),
                      final_norm=(l == depth - 1))
    return h2.reshape(batch, seq, d)
```

```python
import functools

import jax
import jax.numpy as jnp
from jax import lax
from jax.experimental import pallas as pl
from jax.experimental.pallas import tpu as pltpu

F32 = jnp.float32
BF16 = jnp.bfloat16

SSM_GROUPS = 4
SSM_HEAD_DIM = 64
LRU_C = 8.0
RMS_EPS = 1e-6

LANES = 128
SUBLANES = 8
VMEM_LIMIT_BYTES = 56 * 1024 * 1024

PERM_BLOCK = 128
GROUP = PERM_BLOCK // SUBLANES
LRU_CHUNK = 256
ROW_TILE = 256
COL_TILE = 512


def _resident(shape):
    nd = len(shape)
    return pl.BlockSpec(shape, lambda *_: (0,) * nd, pipeline_mode=pl.Buffered(1))


def _softplus(x):
    return jnp.maximum(x, 0.0) + jnp.log1p(jnp.exp(-jnp.abs(x)))


def _silu(x):
    return x * jax.nn.sigmoid(x)


def _rms(x, w):
    ms = jnp.mean(x * x, axis=-1, keepdims=True)
    return x * lax.rsqrt(ms + RMS_EPS) * w


def _split3_bf16(v):
    hi = v.astype(BF16)
    r = v - hi.astype(F32)
    mid = r.astype(BF16)
    lo = (r - mid.astype(F32)).astype(BF16)
    return hi, mid, lo


def _strided_rows(blk, r):
    return pl.ds(blk * PERM_BLOCK + r, GROUP, stride=SUBLANES)


def _permute_rows(slab_ref, value):
    rows, cols = value.shape
    out = []
    for s in range(cols // LANES):
        slab_ref[s] = value[:, s * LANES:(s + 1) * LANES]
        out.append(jnp.concatenate(
            [slab_ref[s, _strided_rows(blk, r), :]
             for blk in range(rows // PERM_BLOCK) for r in range(SUBLANES)], axis=0))
    return jnp.concatenate(out, axis=1)


def _unpermute_rows(slab_ref, value):
    rows, cols = value.shape
    out = []
    for s in range(cols // LANES):
        for blk in range(rows // PERM_BLOCK):
            for r in range(SUBLANES):
                p = blk * PERM_BLOCK + r * GROUP
                slab_ref[s, _strided_rows(blk, r), :] = value[p:p + GROUP, s * LANES:(s + 1) * LANES]
        out.append(slab_ref[s])
    return jnp.concatenate(out, axis=1)


def _shift_rows(v, s, fill):
    n = v.shape[0]
    if s % SUBLANES == 0:
        head = jnp.broadcast_to(jnp.asarray(fill, v.dtype), (s, v.shape[1]))
        return jnp.concatenate([head, v[:n - s]], axis=0)
    t = lax.broadcasted_iota(jnp.int32, v.shape, 0)
    return jnp.where(t >= s, pltpu.roll(v, s, axis=0), fill)


def _perm_conv(cur, tail, cw, cb):
    k_width = cw.shape[0]
    slabs = [cur[GROUP * r:GROUP * (r + 1)] for r in range(SUBLANES)]
    prev_group = {}
    for d in range(1, k_width):
        q = SUBLANES - d
        prev_group[q] = _shift_rows(slabs[q], 1, tail[k_width - 1 - d:k_width - d, :])
    outs = []
    for r in range(SUBLANES):
        acc = cb
        for k in range(k_width):
            d = k_width - 1 - k
            src = slabs[r - d] if r >= d else prev_group[r - d + SUBLANES]
            acc = acc + cw[k:k + 1, :] * src
        outs.append(acc)
    return jnp.concatenate(outs, axis=0)


def _conv_tail(cur, k_width):
    rows = [GROUP * (SUBLANES - d) + GROUP - 1 for d in range(k_width - 1, 0, -1)]
    return jnp.concatenate([cur[p:p + 1] for p in rows], axis=0)


def _perm_scan(a, b, h_prev):
    pa = [a[0:GROUP]]
    pb = [b[0:GROUP]]
    for r in range(1, SUBLANES):
        ar = a[GROUP * r:GROUP * (r + 1)]
        pa.append(ar * pa[-1])
        pb.append(ar * pb[-1] + b[GROUP * r:GROUP * (r + 1)])
    ga, gb = pa[-1], pb[-1]
    s = 1
    while s < GROUP:
        gb = ga * _shift_rows(gb, s, 0.0) + gb
        ga = ga * _shift_rows(ga, s, 1.0)
        s *= 2
    h_end = ga * h_prev + gb
    h_in = _shift_rows(h_end, 1, h_prev)
    h = jnp.concatenate([pa[r] * h_in + pb[r] for r in range(SUBLANES)], axis=0)
    return h, h_end[GROUP - 1:GROUP]


def _inproj_kernel(x_ref, nw_ref, wg_ref, wz_ref, wxbc_ref, wlx_ref, wly_ref, wdt_ref, bg_ref,
                   g_ref, zs_ref, xbc_ref, lx_ref, gy_ref, dt_ref, slab_ref, hn_ref):
    hn_ref[...] = _permute_rows(slab_ref, _rms(x_ref[...], nw_ref[...])).astype(BF16)

    def project(w_ref, out_ref, post):
        n = w_ref.shape[1]
        step = min(COL_TILE, n)
        for lo in range(0, n, step):
            hi = min(lo + step, n)
            acc = jnp.dot(hn_ref[...], w_ref[:, lo:hi], preferred_element_type=F32)
            out_ref[:, lo:hi] = post(acc, lo, hi).astype(out_ref.dtype)

    project(wg_ref, g_ref, lambda a, lo, hi: jax.nn.sigmoid(a + bg_ref[:, lo:hi]))
    project(wz_ref, zs_ref, lambda a, lo, hi: _silu(a))
    project(wxbc_ref, xbc_ref, lambda a, lo, hi: a)
    project(wlx_ref, lx_ref, lambda a, lo, hi: a)
    project(wly_ref, gy_ref, lambda a, lo, hi: jax.nn.gelu(a, approximate=True))
    project(wdt_ref, dt_ref, lambda a, lo, hi: a)


def _in_proj(x2, nw, wg, wz, wxbc, wlx, wly, wdt, bg):
    t, d = x2.shape
    tm = ROW_TILE
    outs = [(wg.shape[1], BF16), (wz.shape[1], BF16), (wxbc.shape[1], BF16),
            (wlx.shape[1], BF16), (wly.shape[1], BF16), (wdt.shape[1], F32)]
    return pl.pallas_call(
        _inproj_kernel,
        grid=(t // tm,),
        in_specs=[pl.BlockSpec((tm, d), lambda i: (i, 0)), _resident(nw.shape),
                  _resident(wg.shape), _resident(wz.shape), _resident(wxbc.shape),
                  _resident(wlx.shape), _resident(wly.shape), _resident(wdt.shape),
                  _resident(bg.shape)],
        out_specs=[pl.BlockSpec((tm, n), lambda i: (i, 0)) for n, _ in outs],
        out_shape=[jax.ShapeDtypeStruct((t, n), dt) for n, dt in outs],
        scratch_shapes=[pltpu.VMEM((d // LANES, tm, LANES), F32), pltpu.VMEM((tm, d), BF16)],
        compiler_params=pltpu.CompilerParams(dimension_semantics=("arbitrary",),
                                             vmem_limit_bytes=VMEM_LIMIT_BYTES),
        name="in_proj",
    )(x2, nw, wg, wz, wxbc, wlx, wly, wdt, bg)


def _ssd_kernel(xbc_ref, zs_ref, dt_ref, cw_ref, cb_ref, dtb_ref, alog_ref, dsk_ref, nw_ref,
                yn_ref, tail_ref, act_ref, st_ref, y_ref, cs_ref, rows_ref):
    lc = PERM_BLOCK
    d_inner = zs_ref.shape[1]
    n_state = (xbc_ref.shape[1] - d_inner) // (2 * SSM_GROUPS)
    heads = d_inner // SSM_HEAD_DIM
    heads_per_group = heads // SSM_GROUPS
    k_width = cw_ref.shape[0]

    @pl.when(pl.program_id(1) == 0)
    def _():
        tail_ref[...] = jnp.zeros(tail_ref.shape, F32)
        st_ref[...] = jnp.zeros(st_ref.shape, F32)

    for lo in range(0, act_ref.shape[1], 2 * LANES):
        cols = slice(lo, lo + 2 * LANES)
        cur = xbc_ref[:, cols].astype(F32)
        act_ref[:, cols] = _silu(_perm_conv(cur, tail_ref[:, cols], cw_ref[:, cols], cb_ref[:, cols]))
        tail_ref[:, cols] = _conv_tail(cur, k_width)

    dt = _softplus(dt_ref[...] + dtb_ref[...])
    d_a = dt * (-jnp.exp(alog_ref[...]))
    ri = lax.broadcasted_iota(jnp.int32, (lc, lc), 0)
    ci = lax.broadcasted_iota(jnp.int32, (lc, lc), 1)
    time_of = lambda p: SUBLANES * (p & (GROUP - 1)) + (p >> (GROUP.bit_length() - 1))
    causal = time_of(ri) >= time_of(ci)
    tri = causal.astype(BF16)
    hi, mid, lo_ = _split3_bf16(d_a)
    cs = (jnp.dot(tri, hi, preferred_element_type=F32)
          + jnp.dot(tri, mid, preferred_element_type=F32)
          + jnp.dot(tri, lo_, preferred_element_type=F32))
    cs_ref[...] = cs
    cs_t = cs.T[:heads]
    dt_t = dt.T[:heads]
    rows_ref[0] = cs_t
    rows_ref[1] = dt_t
    rows_ref[2] = dt_t * jnp.exp(cs_t[:, lc - 1:lc] - cs_t)

    lane = lax.broadcasted_iota(jnp.int32, (1, LANES), 1)
    first_half = lane < SSM_HEAD_DIM

    for g in range(SSM_GROUPS):
        b_off = d_inner + g * n_state
        c_off = d_inner + SSM_GROUPS * n_state + g * n_state
        bm = act_ref[:, b_off:b_off + n_state]
        cm = act_ref[:, c_off:c_off + n_state]
        cb = lax.dot_general(cm.astype(BF16), bm.astype(BF16), (((1,), (1,)), ((), ())),
                             preferred_element_type=F32)
        bm_t = bm.T
        for pair in range(heads_per_group // 2):
            j = g * (heads_per_group // 2) + pair
            cols = slice(j * LANES, (j + 1) * LANES)
            xs = act_ref[:, cols]
            s_old = st_ref[j]
            y_pair = xs * dsk_ref[:, cols]
            s_new = None
            decay_row = None
            for k in range(2):
                h = 2 * j + k
                keep = first_half if k == 0 else jnp.logical_not(first_half)
                col = jnp.broadcast_to(cs_ref[:, h:h + 1], (lc, lc))
                row = rows_ref[0, h:h + 1, :]
                seg = jnp.exp(jnp.where(causal, col - row, -jnp.inf))
                g_mat = (cb * seg * rows_ref[1, h:h + 1, :]).astype(BF16)
                c_mat = (cm * jnp.exp(col)).astype(BF16)
                lhs = jnp.concatenate([g_mat, c_mat], axis=1)
                xs_k = jnp.where(keep, xs, 0.0).astype(BF16)
                rhs = jnp.concatenate([xs_k, jnp.where(keep, s_old, 0.0).astype(BF16)], axis=0)
                y_pair = y_pair + jnp.dot(lhs, rhs, preferred_element_type=F32)
                bw = (bm_t * rows_ref[2, h:h + 1, :]).astype(BF16)
                upd = jnp.dot(bw, xs_k, preferred_element_type=F32)
                s_new = upd if s_new is None else s_new + upd
                tot = jnp.exp(col[lc - 1:lc, :])
                decay_row = tot if decay_row is None else jnp.where(first_half, decay_row, tot)
            y_ref[:, cols] = y_pair
            st_ref[j] = s_old * decay_row + s_new

    gw = d_inner // SSM_GROUPS
    for g in range(SSM_GROUPS):
        cols = slice(g * gw, (g + 1) * gw)
        yg = y_ref[:, cols] * zs_ref[:, cols].astype(F32)
        yn_ref[:, cols] = _rms(yg, nw_ref[:, cols]).astype(yn_ref.dtype)


def _ssd(xbc, zs, dt_raw, cw, cb, dtb, alog, dsk, nw, batch, seq):
    lc = PERM_BLOCK
    nc = seq // lc
    conv_dim = xbc.shape[1]
    d_inner = zs.shape[1]
    n_state = (conv_dim - d_inner) // (2 * SSM_GROUPS)
    heads = d_inner // SSM_HEAD_DIM
    row = lambda b, c: (b * nc + c, 0)
    return pl.pallas_call(
        _ssd_kernel,
        grid=(batch, nc),
        in_specs=[pl.BlockSpec((lc, conv_dim), row), pl.BlockSpec((lc, d_inner), row),
                  pl.BlockSpec((lc, LANES), row), _resident(cw.shape), _resident(cb.shape),
                  _resident(dtb.shape), _resident(alog.shape), _resident(dsk.shape),
                  _resident(nw.shape)],
        out_specs=pl.BlockSpec((lc, d_inner), row),
        out_shape=jax.ShapeDtypeStruct((batch * seq, d_inner), BF16),
        scratch_shapes=[pltpu.VMEM((cw.shape[0] - 1, conv_dim), F32),
                        pltpu.VMEM((lc, conv_dim), F32),
                        pltpu.VMEM((d_inner // LANES, n_state, LANES), F32),
                        pltpu.VMEM((lc, d_inner), F32),
                        pltpu.VMEM((lc, LANES), F32),
                        pltpu.VMEM((3, heads, lc), F32)],
        compiler_params=pltpu.CompilerParams(dimension_semantics=("arbitrary", "arbitrary"),
                                             vmem_limit_bytes=VMEM_LIMIT_BYTES),
        name="ssd_mixer",
    )(xbc, zs, dt_raw, cw, cb, dtb, alog, dsk, nw)


def _lru_kernel(lx_ref, gy_ref, cw_ref, cb_ref, wri_ref, br_ref, bi_ref, lam_ref,
                out_ref, tail_ref, h_ref):
    width = lx_ref.shape[1]
    k_width = cw_ref.shape[0]

    @pl.when(pl.program_id(1) == 0)
    def _():
        tail_ref[...] = jnp.zeros(tail_ref.shape, F32)
        h_ref[...] = jnp.zeros(h_ref.shape, F32)

    for blk in range(LRU_CHUNK // PERM_BLOCK):
        rows = slice(blk * PERM_BLOCK, (blk + 1) * PERM_BLOCK)
        for j in range(width // LANES):
            cols = slice(j * LANES, (j + 1) * LANES)
            cur = lx_ref[rows, cols].astype(F32)
            u = _perm_conv(cur, tail_ref[:, cols], cw_ref[:, cols], cb_ref[:, cols])
            tail_ref[:, cols] = _conv_tail(cur, k_width)
            ri = jnp.dot(u.astype(BF16), wri_ref[j], preferred_element_type=F32)
            r_gate = jax.nn.sigmoid(ri[:, :LANES] + br_ref[:, cols])
            i_gate = jax.nn.sigmoid(ri[:, LANES:] + bi_ref[:, cols])
            log_a = (-LRU_C) * r_gate * _softplus(-lam_ref[:, cols])
            a = jnp.exp(log_a)
            b = jnp.sqrt(1.0 - jnp.exp(2.0 * log_a)) * (i_gate * u)
            h, h_last = _perm_scan(a, b, h_ref[:, cols])
            h_ref[:, cols] = h_last
            out_ref[rows, cols] = (h * gy_ref[rows, cols].astype(F32)).astype(out_ref.dtype)


def _lru(lx, gy, cw, cb, wri, br, bi, lam, batch, seq):
    lc = LRU_CHUNK
    nc = seq // lc
    width = lx.shape[1]
    row = lambda b, c: (b * nc + c, 0)
    return pl.pallas_call(
        _lru_kernel,
        grid=(batch, nc),
        in_specs=[pl.BlockSpec((lc, width), row), pl.BlockSpec((lc, width), row),
                  _resident(cw.shape), _resident(cb.shape), _resident(wri.shape),
                  _resident(br.shape), _resident(bi.shape), _resident(lam.shape)],
        out_specs=pl.BlockSpec((lc, width), row),
        out_shape=jax.ShapeDtypeStruct((batch * seq, width), BF16),
        scratch_shapes=[pltpu.VMEM((cw.shape[0] - 1, width), F32), pltpu.VMEM((1, width), F32)],
        compiler_params=pltpu.CompilerParams(dimension_semantics=("arbitrary", "arbitrary"),
                                             vmem_limit_bytes=VMEM_LIMIT_BYTES),
        name="lru_mixer",
    )(lx, gy, cw, cb, wri, br, bi, lam)


def _out_ffn_kernel(x_ref, yn_ref, lr_ref, g_ref, wos_ref, wol_ref, wo_ref, n2_ref,
                    wfg_ref, wfu_ref, wfo_ref, nf_ref, out_ref, slab_ref, *, final_norm):
    d = x_ref.shape[1]
    y_ssm = jnp.dot(yn_ref[...], wos_ref[...], preferred_element_type=F32)
    y_lru = jnp.dot(lr_ref[...], wol_ref[...], preferred_element_type=F32)
    merged = g_ref[:, :d].astype(F32) * y_ssm + g_ref[:, d:].astype(F32) * y_lru
    mixed = jnp.dot(merged.astype(BF16), wo_ref[...], preferred_element_type=F32)
    h = x_ref[...] + _unpermute_rows(slab_ref, mixed)
    hn = _rms(h, n2_ref[...]).astype(BF16)
    hidden = wfg_ref.shape[1]
    acc = h
    for lo in range(0, hidden, COL_TILE):
        hi = min(lo + COL_TILE, hidden)
        gate = jnp.dot(hn, wfg_ref[:, lo:hi], preferred_element_type=F32)
        up = jnp.dot(hn, wfu_ref[:, lo:hi], preferred_element_type=F32)
        act = (_silu(gate) * up).astype(BF16)
        acc = acc + jnp.dot(act, wfo_ref[lo:hi, :], preferred_element_type=F32)
    if final_norm:
        acc = _rms(acc, nf_ref[...])
    out_ref[...] = acc


def _out_ffn(x2, yn, lr, gates, wos, wol, wo, n2, wfg, wfu, wfo, nf, final_norm):
    t, d = x2.shape
    tm = ROW_TILE
    row = lambda i: (i, 0)
    return pl.pallas_call(
        functools.partial(_out_ffn_kernel, final_norm=final_norm),
        grid=(t // tm,),
        in_specs=[pl.BlockSpec((tm, d), row), pl.BlockSpec((tm, yn.shape[1]), row),
                  pl.BlockSpec((tm, lr.shape[1]), row), pl.BlockSpec((tm, gates.shape[1]), row),
                  _resident(wos.shape), _resident(wol.shape), _resident(wo.shape),
                  _resident(n2.shape), _resident(wfg.shape), _resident(wfu.shape),
                  _resident(wfo.shape), _resident(nf.shape)],
        out_specs=pl.BlockSpec((tm, d), row),
        out_shape=jax.ShapeDtypeStruct((t, d), F32),
        scratch_shapes=[pltpu.VMEM((d // LANES, tm, LANES), F32)],
        compiler_params=pltpu.CompilerParams(dimension_semantics=("arbitrary",),
                                             vmem_limit_bytes=VMEM_LIMIT_BYTES),
        name="out_ffn",
    )(x2, yn, lr, gates, wos, wol, wo, n2, wfg, wfu, wfo, nf)


def _pad_lanes(a):
    pad = (-a.shape[-1]) % LANES
    return jnp.pad(a, [(0, 0)] * (a.ndim - 1) + [(0, pad)])


def kernel(x, norm1_w, w_in, b_branch_gate, ssm_conv_w, ssm_conv_b, ssm_dt_bias, ssm_a_log, ssm_d, ssm_norm_w, w_out_ssm, lru_conv_w, lru_conv_b, lru_w_r, lru_b_r, lru_w_i, lru_b_i, lru_lambda, w_out_lru, w_out, norm2_w, w_ffn_in, w_ffn_out, norm_f_w):
    batch, seq, d = x.shape
    depth = norm1_w.shape[0]
    d_inner = w_out_ssm.shape[1]
    conv_dim = ssm_conv_w.shape[2]
    heads = ssm_dt_bias.shape[1]
    width = w_out_lru.shape[1]
    hidden = w_ffn_out.shape[1]
    n_gates = b_branch_gate.shape[1]
    assert seq % LRU_CHUNK == 0 and LRU_CHUNK % PERM_BLOCK == 0 and ROW_TILE % PERM_BLOCK == 0
    assert (batch * seq) % ROW_TILE == 0 and seq % ROW_TILE == 0
    assert d_inner == heads * SSM_HEAD_DIM and width % LANES == 0 and n_gates == 2 * d

    o_z = n_gates
    o_xbc = o_z + d_inner
    o_dt = o_xbc + conv_dim
    o_lx = o_dt + heads
    o_ly = o_lx + width

    h2 = x.reshape(batch * seq, d)
    for l in range(depth):
        wi = w_in[l].astype(BF16)
        row2 = lambda a: a[l].reshape(1, -1)
        g, zs, xbc, lx, gy, dt_raw = _in_proj(
            h2, row2(norm1_w), wi[:, :o_z], wi[:, o_z:o_xbc], wi[:, o_xbc:o_dt],
            wi[:, o_lx:o_ly], wi[:, o_ly:], _pad_lanes(wi[:, o_dt:o_lx]), row2(b_branch_gate))
        yn = _ssd(xbc, zs, dt_raw, ssm_conv_w[l], row2(ssm_conv_b), _pad_lanes(row2(ssm_dt_bias)),
                  _pad_lanes(row2(ssm_a_log)), jnp.repeat(ssm_d[l], SSM_HEAD_DIM).reshape(1, -1),
                  row2(ssm_norm_w), batch, seq)
        wri = jnp.concatenate([lru_w_r[l], lru_w_i[l]], axis=-1).astype(BF16)
        lr = _lru(lx, gy, lru_conv_w[l], row2(lru_conv_b), wri, row2(lru_b_r), row2(lru_b_i),
                  row2(lru_lambda), batch, seq)
        wf = w_ffn_in[l].astype(BF16)
        h2 = _out_ffn(h2, yn, lr, g, w_out_ssm[l].astype(BF16), w_out_lru[l].astype(BF16),
                      w_out[l].astype(BF16), row2(norm2_w), wf[:, :hidden], wf[:, hidden:],
                      w_ffn_out[l].astype(BF16), norm_f_w.reshape(1, -1),
                      final_norm=(l == depth - 1))
    return h2.reshape(batch, seq, d)
```

```python
import functools

import jax
import jax.numpy as jnp
from jax import lax
from jax.experimental import pallas as pl
from jax.experimental.pallas import tpu as pltpu

F32 = jnp.float32
BF16 = jnp.bfloat16

SSM_GROUPS = 4
SSM_HEAD_DIM = 64
LRU_C = 8.0
RMS_EPS = 1e-6

LANES = 128
SUBLANES = 8
VMEM_LIMIT_BYTES = 56 * 1024 * 1024

PERM_BLOCK = 128
GROUP = PERM_BLOCK // SUBLANES
STEP_TOKENS = 256
STEP_BLOCKS = STEP_TOKENS // PERM_BLOCK
FFN_ROWS = 256
COL_TILE = 512


def _resident(shape):
    nd = len(shape)
    return pl.BlockSpec(shape, lambda *_: (0,) * nd, pipeline_mode=pl.Buffered(1))


def _softplus(x):
    return jnp.maximum(x, 0.0) + jnp.log1p(jnp.exp(-jnp.abs(x)))


def _silu(x):
    return x * jax.nn.sigmoid(x)


def _rms(x, w):
    ms = jnp.mean(x * x, axis=-1, keepdims=True)
    return x * lax.rsqrt(ms + RMS_EPS) * w


def _split3_bf16(v):
    hi = v.astype(BF16)
    r = v - hi.astype(F32)
    mid = r.astype(BF16)
    lo = (r - mid.astype(F32)).astype(BF16)
    return hi, mid, lo


def _dot(a, b):
    return jnp.dot(a, b, preferred_element_type=F32)


def _strided_rows(blk, r):
    return pl.ds(blk * PERM_BLOCK + r, GROUP, stride=SUBLANES)


def _permute_rows(slab_ref, value):
    rows, cols = value.shape
    out = []
    for s in range(cols // LANES):
        slab_ref[s] = value[:, s * LANES:(s + 1) * LANES]
        out.append(jnp.concatenate(
            [slab_ref[s, _strided_rows(blk, r), :]
             for blk in range(rows // PERM_BLOCK) for r in range(SUBLANES)], axis=0))
    return jnp.concatenate(out, axis=1)


def _unpermute_rows(slab_ref, value):
    rows, cols = value.shape
    out = []
    for s in range(cols // LANES):
        for blk in range(rows // PERM_BLOCK):
            for r in range(SUBLANES):
                p = blk * PERM_BLOCK + r * GROUP
                slab_ref[s, _strided_rows(blk, r), :] = value[p:p + GROUP, s * LANES:(s + 1) * LANES]
        out.append(slab_ref[s])
    return jnp.concatenate(out, axis=1)


def _shift_rows(v, s, fill):
    n = v.shape[0]
    if s % SUBLANES == 0:
        head = jnp.broadcast_to(jnp.asarray(fill, v.dtype), (s, v.shape[1]))
        return jnp.concatenate([head, v[:n - s]], axis=0)
    t = lax.broadcasted_iota(jnp.int32, v.shape, 0)
    return jnp.where(t >= s, pltpu.roll(v, s, axis=0), fill)


def _perm_conv(cur, tail, cw, cb):
    k_width = cw.shape[0]
    slabs = [cur[GROUP * r:GROUP * (r + 1)] for r in range(SUBLANES)]
    prev_group = {}
    for d in range(1, k_width):
        q = SUBLANES - d
        prev_group[q] = _shift_rows(slabs[q], 1, tail[k_width - 1 - d:k_width - d, :])
    outs = []
    for r in range(SUBLANES):
        acc = cb
        for k in range(k_width):
            d = k_width - 1 - k
            src = slabs[r - d] if r >= d else prev_group[r - d + SUBLANES]
            acc = acc + cw[k:k + 1, :] * src
        outs.append(acc)
    return jnp.concatenate(outs, axis=0)


def _conv_tail(cur, k_width):
    rows = [GROUP * (SUBLANES - d) + GROUP - 1 for d in range(k_width - 1, 0, -1)]
    return jnp.concatenate([cur[p:p + 1] for p in rows], axis=0)


def _perm_scan(a, b, h_prev):
    pa = [a[0:GROUP]]
    pb = [b[0:GROUP]]
    for r in range(1, SUBLANES):
        ar = a[GROUP * r:GROUP * (r + 1)]
        pa.append(ar * pa[-1])
        pb.append(ar * pb[-1] + b[GROUP * r:GROUP * (r + 1)])
    ga, gb = pa[-1], pb[-1]
    s = 1
    while s < GROUP:
        gb = ga * _shift_rows(gb, s, 0.0) + gb
        ga = ga * _shift_rows(ga, s, 1.0)
        s *= 2
    h_end = ga * h_prev + gb
    h_in = _shift_rows(h_end, 1, h_prev)
    h = jnp.concatenate([pa[r] * h_in + pb[r] for r in range(SUBLANES)], axis=0)
    return h, h_end[GROUP - 1:GROUP]


def _ssd_chunk(act_ref, rows, dt_raw, dtb, a_log, dsk_ref, st_ref, y_ref, cs_ref, rows_ref,
               d_inner, n_state):
    lc = PERM_BLOCK
    heads = d_inner // SSM_HEAD_DIM
    heads_per_group = heads // SSM_GROUPS

    dt = _softplus(dt_raw + dtb)
    d_a = dt * (-jnp.exp(a_log))
    ri = lax.broadcasted_iota(jnp.int32, (lc, lc), 0)
    ci = lax.broadcasted_iota(jnp.int32, (lc, lc), 1)
    time_of = lambda p: SUBLANES * (p & (GROUP - 1)) + (p >> (GROUP.bit_length() - 1))
    causal = time_of(ri) >= time_of(ci)
    tri = causal.astype(BF16)
    hi, mid, lo_ = _split3_bf16(d_a)
    cs = _dot(tri, hi) + _dot(tri, mid) + _dot(tri, lo_)
    cs_ref[...] = cs
    cs_t = cs.T[:heads]
    dt_t = dt.T[:heads]
    rows_ref[0] = cs_t
    rows_ref[1] = dt_t
    rows_ref[2] = dt_t * jnp.exp(cs_t[:, lc - 1:lc] - cs_t)

    lane = lax.broadcasted_iota(jnp.int32, (1, LANES), 1)
    first_half = lane < SSM_HEAD_DIM

    for g in range(SSM_GROUPS):
        b_off = d_inner + g * n_state
        c_off = d_inner + SSM_GROUPS * n_state + g * n_state
        bm = act_ref[rows, b_off:b_off + n_state]
        cm = act_ref[rows, c_off:c_off + n_state]
        cb = lax.dot_general(cm.astype(BF16), bm.astype(BF16), (((1,), (1,)), ((), ())),
                             preferred_element_type=F32)
        bm_t = bm.T
        for pair in range(heads_per_group // 2):
            j = g * (heads_per_group // 2) + pair
            cols = slice(j * LANES, (j + 1) * LANES)
            xs = act_ref[rows, cols]
            s_old = st_ref[j]
            y_pair = xs * dsk_ref[:, cols]
            s_new = None
            decay_row = None
            for k in range(2):
                h = 2 * j + k
                keep = first_half if k == 0 else jnp.logical_not(first_half)
                col = jnp.broadcast_to(cs_ref[:, h:h + 1], (lc, lc))
                row = rows_ref[0, h:h + 1, :]
                seg = jnp.exp(jnp.where(causal, col - row, -jnp.inf))
                g_mat = (cb * seg * rows_ref[1, h:h + 1, :]).astype(BF16)
                c_mat = (cm * jnp.exp(col)).astype(BF16)
                lhs = jnp.concatenate([g_mat, c_mat], axis=1)
                xs_k = jnp.where(keep, xs, 0.0).astype(BF16)
                rhs = jnp.concatenate([xs_k, jnp.where(keep, s_old, 0.0).astype(BF16)], axis=0)
                y_pair = y_pair + _dot(lhs, rhs)
                bw = (bm_t * rows_ref[2, h:h + 1, :]).astype(BF16)
                upd = _dot(bw, xs_k)
                s_new = upd if s_new is None else s_new + upd
                tot = jnp.exp(col[lc - 1:lc, :])
                decay_row = tot if decay_row is None else jnp.where(first_half, decay_row, tot)
            y_ref[rows, cols] = y_pair
            st_ref[j] = s_old * decay_row + s_new


def _mixer_kernel(x_ref, nw1_ref, wg_ref, bg_ref, wz_ref, wxbc_ref, wdt_ref, wlx_ref, wly_ref,
                  cw_ref, cb_ref, dtb_ref, alog_ref, dsk_ref, snw_ref,
                  lcw_ref, lcb_ref, wri_ref, br_ref, bi_ref, lam_ref,
                  wos_ref, wol_ref, wo_ref,
                  out_ref,
                  slab_ref, hn_ref, act_ref, y_ref, yn_ref, lr_ref, mg_ref,
                  ctail_ref, st_ref, cs_ref, rows_ref, ltail_ref, h_ref):
    d = x_ref.shape[1]
    d_inner = wz_ref.shape[1]
    conv_dim = wxbc_ref.shape[1]
    n_state = (conv_dim - d_inner) // (2 * SSM_GROUPS)
    width = wlx_ref.shape[1]
    k_ssm = cw_ref.shape[0]
    k_lru = lcw_ref.shape[0]
    blocks = [slice(b * PERM_BLOCK, (b + 1) * PERM_BLOCK) for b in range(STEP_BLOCKS)]

    @pl.when(pl.program_id(1) == 0)
    def _():
        ctail_ref[...] = jnp.zeros(ctail_ref.shape, F32)
        st_ref[...] = jnp.zeros(st_ref.shape, F32)
        ltail_ref[...] = jnp.zeros(ltail_ref.shape, F32)
        h_ref[...] = jnp.zeros(h_ref.shape, F32)

    hn_ref[...] = _permute_rows(slab_ref, _rms(x_ref[...], nw1_ref[...])).astype(BF16)

    for lo in range(0, conv_dim, COL_TILE):
        cols = slice(lo, lo + COL_TILE)
        proj = _dot(hn_ref[...], wxbc_ref[:, cols])
        for rows in blocks:
            cur = proj[rows]
            act_ref[rows, cols] = _silu(_perm_conv(cur, ctail_ref[:, cols], cw_ref[:, cols], cb_ref[:, cols]))
            ctail_ref[:, cols] = _conv_tail(cur, k_ssm)
    dt_raw = _dot(hn_ref[...], wdt_ref[...])
    for b, rows in enumerate(blocks):
        _ssd_chunk(act_ref, rows, dt_raw[rows], dtb_ref[...], alog_ref[...], dsk_ref, st_ref,
                   y_ref, cs_ref.at[b], rows_ref.at[b], d_inner, n_state)
    gw = d_inner // SSM_GROUPS
    for g in range(SSM_GROUPS):
        cols = slice(g * gw, (g + 1) * gw)
        yg = y_ref[:, cols] * _silu(_dot(hn_ref[...], wz_ref[:, cols]))
        yn_ref[:, cols] = _rms(yg, snw_ref[:, cols]).astype(BF16)

    for lo in range(0, width, 2 * LANES):
        pcols = slice(lo, lo + 2 * LANES)
        proj_x = _dot(hn_ref[...], wlx_ref[:, pcols])
        gate_y = jax.nn.gelu(_dot(hn_ref[...], wly_ref[:, pcols]), approximate=True)
        for half in range(2):
            j = lo // LANES + half
            cols = slice(j * LANES, (j + 1) * LANES)
            hcols = slice(half * LANES, (half + 1) * LANES)
            u_blocks = []
            for rows in blocks:
                cur = proj_x[rows, hcols]
                u_blocks.append(_perm_conv(cur, ltail_ref[:, cols], lcw_ref[:, cols], lcb_ref[:, cols]))
                ltail_ref[:, cols] = _conv_tail(cur, k_lru)
            u = jnp.concatenate(u_blocks, axis=0)
            ri = _dot(u.astype(BF16), wri_ref[j])
            r_gate = jax.nn.sigmoid(ri[:, :LANES] + br_ref[:, cols])
            i_gate = jax.nn.sigmoid(ri[:, LANES:] + bi_ref[:, cols])
            log_a = (-LRU_C) * r_gate * _softplus(-lam_ref[:, cols])
            a = jnp.exp(log_a)
            b_in = jnp.sqrt(1.0 - jnp.exp(2.0 * log_a)) * (i_gate * u)
            for rows in blocks:
                h, h_last = _perm_scan(a[rows], b_in[rows], h_ref[:, cols])
                h_ref[:, cols] = h_last
                lr_ref[rows, cols] = (h * gate_y[rows, hcols]).astype(BF16)

    for lo in range(0, d, COL_TILE):
        cols = slice(lo, lo + COL_TILE)
        gcols = slice(d + lo, d + lo + COL_TILE)
        g_ssm = jax.nn.sigmoid(_dot(hn_ref[...], wg_ref[:, cols]) + bg_ref[:, cols])
        g_lru = jax.nn.sigmoid(_dot(hn_ref[...], wg_ref[:, gcols]) + bg_ref[:, gcols])
        merged = (g_ssm * _dot(yn_ref[...], wos_ref[:, cols])
                  + g_lru * _dot(lr_ref[...], wol_ref[:, cols]))
        mg_ref[:, cols] = merged.astype(BF16)
    mixed = _dot(mg_ref[...], wo_ref[...])
    out_ref[...] = x_ref[...] + _unpermute_rows(slab_ref, mixed)


def _mixer(x2, batch, seq, weights):
    t, d = x2.shape
    tm = STEP_TOKENS
    nc = seq // tm
    (nw1, wg, bg, wz, wxbc, wdt, wlx, wly, cw, cb, dtb, alog, dsk, snw,
     lcw, lcb, wri, br, bi, lam, wos, wol, wo) = weights
    d_inner = wz.shape[1]
    conv_dim = wxbc.shape[1]
    n_state = (conv_dim - d_inner) // (2 * SSM_GROUPS)
    heads = d_inner // SSM_HEAD_DIM
    width = wlx.shape[1]
    row = lambda b, c: (b * nc + c, 0)
    return pl.pallas_call(
        _mixer_kernel,
        grid=(batch, nc),
        in_specs=[pl.BlockSpec((tm, d), row)] + [_resident(w.shape) for w in weights],
        out_specs=pl.BlockSpec((tm, d), row),
        out_shape=jax.ShapeDtypeStruct((t, d), F32),
        scratch_shapes=[
            pltpu.VMEM((d // LANES, tm, LANES), F32),
            pltpu.VMEM((tm, d), BF16),
            pltpu.VMEM((tm, conv_dim), F32),
            pltpu.VMEM((tm, d_inner), F32),
            pltpu.VMEM((tm, d_inner), BF16),
            pltpu.VMEM((tm, width), BF16),
            pltpu.VMEM((tm, d), BF16),
            pltpu.VMEM((cw.shape[0] - 1, conv_dim), F32),
            pltpu.VMEM((d_inner // LANES, n_state, LANES), F32),
            pltpu.VMEM((STEP_BLOCKS, PERM_BLOCK, LANES), F32),
            pltpu.VMEM((STEP_BLOCKS, 3, heads, PERM_BLOCK), F32),
            pltpu.VMEM((lcw.shape[0] - 1, width), F32),
            pltpu.VMEM((1, width), F32),
        ],
        compiler_params=pltpu.CompilerParams(dimension_semantics=("arbitrary", "arbitrary"),
                                             vmem_limit_bytes=VMEM_LIMIT_BYTES),
        name="mixer",
    )(x2, *weights)


def _ffn_kernel(h_ref, n2_ref, wfg_ref, wfu_ref, wfo_ref, nf_ref, out_ref, *, final_norm):
    h = h_ref[...]
    hn = _rms(h, n2_ref[...]).astype(BF16)
    hidden = wfg_ref.shape[1]
    acc = h
    for lo in range(0, hidden, COL_TILE):
        hi = min(lo + COL_TILE, hidden)
        act = (_silu(_dot(hn, wfg_ref[:, lo:hi])) * _dot(hn, wfu_ref[:, lo:hi])).astype(BF16)
        acc = acc + _dot(act, wfo_ref[lo:hi, :])
    if final_norm:
        acc = _rms(acc, nf_ref[...])
    out_ref[...] = acc


def _ffn(h2, n2, wfg, wfu, wfo, nf, final_norm):
    t, d = h2.shape
    tm = FFN_ROWS
    row = lambda i: (i, 0)
    weights = (n2, wfg, wfu, wfo, nf)
    return pl.pallas_call(
        functools.partial(_ffn_kernel, final_norm=final_norm),
        grid=(t // tm,),
        in_specs=[pl.BlockSpec((tm, d), row)] + [_resident(w.shape) for w in weights],
        out_specs=pl.BlockSpec((tm, d), row),
        out_shape=jax.ShapeDtypeStruct((t, d), F32),
        compiler_params=pltpu.CompilerParams(dimension_semantics=("arbitrary",),
                                             vmem_limit_bytes=VMEM_LIMIT_BYTES),
        name="ffn",
    )(h2, *weights)


def _pad_lanes(a):
    pad = (-a.shape[-1]) % LANES
    return jnp.pad(a, [(0, 0)] * (a.ndim - 1) + [(0, pad)])


def kernel(x, norm1_w, w_in, b_branch_gate, ssm_conv_w, ssm_conv_b, ssm_dt_bias, ssm_a_log, ssm_d, ssm_norm_w, w_out_ssm, lru_conv_w, lru_conv_b, lru_w_r, lru_b_r, lru_w_i, lru_b_i, lru_lambda, w_out_lru, w_out, norm2_w, w_ffn_in, w_ffn_out, norm_f_w):
    batch, seq, d = x.shape
    depth = norm1_w.shape[0]
    d_inner = w_out_ssm.shape[1]
    conv_dim = ssm_conv_w.shape[2]
    heads = ssm_dt_bias.shape[1]
    width = w_out_lru.shape[1]
    hidden = w_ffn_out.shape[1]
    n_gates = b_branch_gate.shape[1]
    assert seq % STEP_TOKENS == 0 and (batch * seq) % FFN_ROWS == 0
    assert d_inner == heads * SSM_HEAD_DIM and n_gates == 2 * d
    assert d % COL_TILE == 0 and conv_dim % COL_TILE == 0 and width % (2 * LANES) == 0
    assert (d_inner // SSM_GROUPS) % LANES == 0

    o_z = n_gates
    o_xbc = o_z + d_inner
    o_dt = o_xbc + conv_dim
    o_lx = o_dt + heads
    o_ly = o_lx + width

    h2 = x.reshape(batch * seq, d)
    for l in range(depth):
        wi = w_in[l].astype(BF16)
        row2 = lambda a: a[l].reshape(1, -1)
        weights = (
            row2(norm1_w), wi[:, :o_z], row2(b_branch_gate), wi[:, o_z:o_xbc], wi[:, o_xbc:o_dt],
            _pad_lanes(wi[:, o_dt:o_lx]), wi[:, o_lx:o_ly], wi[:, o_ly:],
            ssm_conv_w[l], row2(ssm_conv_b), _pad_lanes(row2(ssm_dt_bias)), _pad_lanes(row2(ssm_a_log)),
            jnp.repeat(ssm_d[l], SSM_HEAD_DIM).reshape(1, -1), row2(ssm_norm_w),
            lru_conv_w[l], row2(lru_conv_b),
            jnp.concatenate([lru_w_r[l], lru_w_i[l]], axis=-1).astype(BF16),
            row2(lru_b_r), row2(lru_b_i), row2(lru_lambda),
            w_out_ssm[l].astype(BF16), w_out_lru[l].astype(BF16), w_out[l].astype(BF16))
        h2 = _mixer(h2, batch, seq, weights)
        wf = w_ffn_in[l].astype(BF16)
        h2 = _ffn(h2, row2(norm2_w), wf[:, :hidden], wf[:, hidden:], w_ffn_out[l].astype(BF16),
                  norm_f_w.reshape(1, -1), final_norm=(l == depth - 1))
    return h2.reshape(batch, seq, d)
```

```python
import functools

import jax
import jax.numpy as jnp
from jax import lax
from jax.experimental import pallas as pl
from jax.experimental.pallas import tpu as pltpu

F32 = jnp.float32
BF16 = jnp.bfloat16

SSM_GROUPS = 4
SSM_HEAD_DIM = 64
LRU_C = 8.0
RMS_EPS = 1e-6
LOG2_E = 1.4426950408889634

LANES = 128
SUBLANES = 8
VMEM_LIMIT_BYTES = 56 * 1024 * 1024

PERM_BLOCK = 128
GROUP = PERM_BLOCK // SUBLANES
STEP_TOKENS = 256
STEP_BLOCKS = STEP_TOKENS // PERM_BLOCK
FFN_ROWS = 512
COL_TILE = 512
SIDE_TILE = 256


def _resident(shape):
    nd = len(shape)
    return pl.BlockSpec(shape, lambda *_: (0,) * nd, pipeline_mode=pl.Buffered(1))


def _softplus(x):
    return jnp.maximum(x, 0.0) + jnp.log1p(jnp.exp(-jnp.abs(x)))


def _sigmoid(x):
    return 0.5 * jnp.tanh(0.5 * x) + 0.5


def _silu(x):
    hx = 0.5 * x
    return hx * jnp.tanh(hx) + hx


def _sqrt_unit(x):
    return jnp.where(x > 0.0, x * lax.rsqrt(x), 0.0)


def _rms(x, w):
    ms = jnp.mean(x * x, axis=-1, keepdims=True)
    return x * lax.rsqrt(ms + RMS_EPS) * w


def _split3_bf16(v):
    hi = v.astype(BF16)
    r = v - hi.astype(F32)
    mid = r.astype(BF16)
    lo = (r - mid.astype(F32)).astype(BF16)
    return hi, mid, lo


def _dot(a, b):
    return jnp.dot(a, b, preferred_element_type=F32)


def _strided_rows(blk, r):
    return pl.ds(blk * PERM_BLOCK + r, GROUP, stride=SUBLANES)


def _permute_rows(slab_ref, value):
    rows, cols = value.shape
    out = []
    for s in range(cols // LANES):
        slab_ref[s] = value[:, s * LANES:(s + 1) * LANES]
        out.append(jnp.concatenate(
            [slab_ref[s, _strided_rows(blk, r), :]
             for blk in range(rows // PERM_BLOCK) for r in range(SUBLANES)], axis=0))
    return jnp.concatenate(out, axis=1)


def _unpermute_rows(slab_ref, value):
    rows, cols = value.shape
    out = []
    for s in range(cols // LANES):
        for blk in range(rows // PERM_BLOCK):
            for r in range(SUBLANES):
                p = blk * PERM_BLOCK + r * GROUP
                slab_ref[s, _strided_rows(blk, r), :] = value[p:p + GROUP, s * LANES:(s + 1) * LANES]
        out.append(slab_ref[s])
    return jnp.concatenate(out, axis=1)


def _shift_rows(v, s, fill):
    n = v.shape[0]
    if s % SUBLANES == 0:
        head = jnp.broadcast_to(jnp.asarray(fill, v.dtype), (s, v.shape[1]))
        return jnp.concatenate([head, v[:n - s]], axis=0)
    t = lax.broadcasted_iota(jnp.int32, v.shape, 0)
    return jnp.where(t >= s, pltpu.roll(v, s, axis=0), fill)


def _perm_conv(cur, tail, cw, cb):
    k_width = cw.shape[0]
    slabs = [cur[GROUP * r:GROUP * (r + 1)] for r in range(SUBLANES)]
    prev_group = {}
    for d in range(1, k_width):
        q = SUBLANES - d
        prev_group[q] = _shift_rows(slabs[q], 1, tail[k_width - 1 - d:k_width - d, :])
    outs = []
    for r in range(SUBLANES):
        acc = cb
        for k in range(k_width):
            d = k_width - 1 - k
            src = slabs[r - d] if r >= d else prev_group[r - d + SUBLANES]
            acc = acc + cw[k:k + 1, :] * src
        outs.append(acc)
    return jnp.concatenate(outs, axis=0)


def _conv_tail(cur, k_width):
    rows = [GROUP * (SUBLANES - d) + GROUP - 1 for d in range(k_width - 1, 0, -1)]
    return jnp.concatenate([cur[p:p + 1] for p in rows], axis=0)


def _perm_scan(a, b, h_prev):
    pa = [a[0:GROUP]]
    pb = [b[0:GROUP]]
    for r in range(1, SUBLANES):
        ar = a[GROUP * r:GROUP * (r + 1)]
        pa.append(ar * pa[-1])
        pb.append(ar * pb[-1] + b[GROUP * r:GROUP * (r + 1)])
    ga, gb = pa[-1], pb[-1]
    s = 1
    while s < GROUP:
        gb = ga * _shift_rows(gb, s, 0.0) + gb
        ga = ga * _shift_rows(ga, s, 1.0)
        s *= 2
    h_end = ga * h_prev + gb
    h_in = _shift_rows(h_end, 1, h_prev)
    h = jnp.concatenate([pa[r] * h_in + pb[r] for r in range(SUBLANES)], axis=0)
    return h, h_end[GROUP - 1:GROUP]


def _causal_mask():
    lc = PERM_BLOCK
    ri = lax.broadcasted_iota(jnp.int32, (lc, lc), 0)
    ci = lax.broadcasted_iota(jnp.int32, (lc, lc), 1)
    time_of = lambda p: SUBLANES * (p & (GROUP - 1)) + (p >> (GROUP.bit_length() - 1))
    return time_of(ri) >= time_of(ci)


def _ssd_decays(dt_raw, dtb, a_log, heads, causal, cs_ref, rows_ref):
    lc = PERM_BLOCK
    dt = _softplus(dt_raw + dtb)
    d_a = dt * (-jnp.exp(a_log))
    tri = causal.astype(BF16)
    hi, mid, lo = _split3_bf16(d_a)
    cs = (_dot(tri, hi) + _dot(tri, mid) + _dot(tri, lo)) * LOG2_E
    cs_ref[...] = cs
    cs_t = cs.T[:heads]
    dt_t = dt.T[:heads]
    rows_ref[0] = cs_t - jnp.log2(dt_t)
    rows_ref[1] = dt_t * jnp.exp2(cs_t[:, lc - 1:lc] - cs_t)


def _ssd_group(act_ref, rows, g, d_inner, n_state):
    b_off = d_inner + g * n_state
    c_off = d_inner + SSM_GROUPS * n_state + g * n_state
    bm = act_ref[rows, b_off:b_off + n_state]
    cm = act_ref[rows, c_off:c_off + n_state]
    cb = lax.dot_general(cm.astype(BF16), bm.astype(BF16), (((1,), (1,)), ((), ())),
                         preferred_element_type=F32)
    return cm, cb, bm.T


def _ssd_pair(act_ref, rows, j, group, causal, first_half, dsk_ref, st_ref, y_ref, cs_ref, rows_ref):
    lc = PERM_BLOCK
    cm, cb, bm_t = group
    cols = slice(j * LANES, (j + 1) * LANES)
    xs = act_ref[rows, cols]
    s_old = st_ref[j]
    y_pair = xs * dsk_ref[:, cols]
    s_new = None
    decay_row = None
    for k in range(2):
        h = 2 * j + k
        keep = first_half if k == 0 else jnp.logical_not(first_half)
        col = jnp.broadcast_to(cs_ref[:, h:h + 1], (lc, lc))
        row = rows_ref[0, h:h + 1, :]
        seg = jnp.exp2(jnp.where(causal, col - row, -jnp.inf))
        g_mat = (cb * seg).astype(BF16)
        c_mat = (cm * jnp.exp2(col)).astype(BF16)
        lhs = jnp.concatenate([g_mat, c_mat], axis=1)
        xs_k = jnp.where(keep, xs, 0.0).astype(BF16)
        rhs = jnp.concatenate([xs_k, jnp.where(keep, s_old, 0.0).astype(BF16)], axis=0)
        y_pair = y_pair + _dot(lhs, rhs)
        bw = (bm_t * rows_ref[1, h:h + 1, :]).astype(BF16)
        upd = _dot(bw, xs_k)
        s_new = upd if s_new is None else s_new + upd
        tot = jnp.exp2(col[lc - 1:lc, :])
        decay_row = tot if decay_row is None else jnp.where(first_half, decay_row, tot)
    y_ref[rows, cols] = y_pair
    st_ref[j] = s_old * decay_row + s_new


def _mixer_kernel(x_ref, nw1_ref, wg_ref, bg_ref, wz_ref, wxbc_ref, wdt_ref, wlx_ref, wly_ref,
                  cw_ref, cb_ref, dtb_ref, alog_ref, dsk_ref, snw_ref,
                  lcw_ref, lcb_ref, wri_ref, br_ref, bi_ref, lam_ref,
                  wos_ref, wol_ref, wo_ref,
                  out_ref,
                  slab_ref, hn_ref, act_ref, y_ref, yn_ref, lr_ref, mg_ref, lx_ref, gy_ref,
                  zs_ref, g_ref, ctail_ref, st_ref, cs_ref, rows_ref, ltail_ref, h_ref, sp_ref):
    d = x_ref.shape[1]
    d_inner = wz_ref.shape[1]
    conv_dim = wxbc_ref.shape[1]
    n_state = (conv_dim - d_inner) // (2 * SSM_GROUPS)
    heads = d_inner // SSM_HEAD_DIM
    pairs_per_group = heads // SSM_GROUPS // 2
    width = wlx_ref.shape[1]
    k_ssm = cw_ref.shape[0]
    k_lru = lcw_ref.shape[0]
    blocks = [slice(b * PERM_BLOCK, (b + 1) * PERM_BLOCK) for b in range(STEP_BLOCKS)]

    @pl.when(pl.program_id(1) == 0)
    def _():
        ctail_ref[...] = jnp.zeros(ctail_ref.shape, F32)
        st_ref[...] = jnp.zeros(st_ref.shape, F32)
        ltail_ref[...] = jnp.zeros(ltail_ref.shape, F32)
        h_ref[...] = jnp.zeros(h_ref.shape, F32)
        sp_ref[...] = _softplus(-lam_ref[...])

    hn_ref[...] = _permute_rows(slab_ref, _rms(x_ref[...], nw1_ref[...])).astype(BF16)

    def project_lru(lo):
        pcols = slice(lo, lo + 2 * LANES)
        lx_ref[:, pcols] = _dot(hn_ref[...], wlx_ref[:, pcols])
        gy_ref[:, pcols] = jax.nn.gelu(_dot(hn_ref[...], wly_ref[:, pcols]), approximate=True).astype(BF16)

    def conv_xbc(lo):
        cols = slice(lo, lo + COL_TILE)
        proj = _dot(hn_ref[...], wxbc_ref[:, cols])
        for rows in blocks:
            cur = proj[rows]
            act_ref[rows, cols] = _silu(_perm_conv(cur, ctail_ref[:, cols], cw_ref[:, cols], cb_ref[:, cols]))
            ctail_ref[:, cols] = _conv_tail(cur, k_ssm)

    def lru_block(j):
        cols = slice(j * LANES, (j + 1) * LANES)
        u_blocks = []
        for rows in blocks:
            cur = lx_ref[rows, cols]
            u_blocks.append(_perm_conv(cur, ltail_ref[:, cols], lcw_ref[:, cols], lcb_ref[:, cols]))
            ltail_ref[:, cols] = _conv_tail(cur, k_lru)
        u = jnp.concatenate(u_blocks, axis=0)
        ri = _dot(u.astype(BF16), wri_ref[j])
        r_gate = _sigmoid(ri[:, :LANES] + br_ref[:, cols])
        i_gate = _sigmoid(ri[:, LANES:] + bi_ref[:, cols])
        log_a = (-LRU_C) * r_gate * sp_ref[:, cols]
        a = jnp.exp(log_a)
        b_in = _sqrt_unit(1.0 - jnp.exp(2.0 * log_a)) * (i_gate * u)
        for rows in blocks:
            h, h_last = _perm_scan(a[rows], b_in[rows], h_ref[:, cols])
            h_ref[:, cols] = h_last
            lr_ref[rows, cols] = (h * gy_ref[rows, cols].astype(F32)).astype(BF16)

    lru_slices = list(range(0, width, 2 * LANES))
    xbc_slices = list(range(0, conv_dim, COL_TILE))
    project_lru(lru_slices[0])
    for i in range(max(len(lru_slices), len(xbc_slices))):
        if i + 1 < len(lru_slices):
            project_lru(lru_slices[i + 1])
        if i < len(xbc_slices):
            conv_xbc(xbc_slices[i])
        if i < len(lru_slices):
            lru_block(lru_slices[i] // LANES)
            lru_block(lru_slices[i] // LANES + 1)

    def side_z(lo):
        cols = slice(lo, lo + SIDE_TILE)
        zs_ref[:, cols] = _silu(_dot(hn_ref[...], wz_ref[:, cols])).astype(BF16)

    def side_g(lo):
        cols = slice(lo, lo + SIDE_TILE)
        g_ref[:, cols] = _sigmoid(_dot(hn_ref[...], wg_ref[:, cols]) + bg_ref[:, cols]).astype(BF16)

    side = ([functools.partial(side_z, lo) for lo in range(0, d_inner, SIDE_TILE)]
            + [functools.partial(side_g, lo) for lo in range(0, 2 * d, SIDE_TILE)])
    n_pairs = STEP_BLOCKS * SSM_GROUPS * pairs_per_group
    dt_raw = _dot(hn_ref[...], wdt_ref[...])
    pairs_done = 0
    side_done = 0
    causal = _causal_mask()
    first_half = lax.broadcasted_iota(jnp.int32, (1, LANES), 1) < SSM_HEAD_DIM
    for b, rows in enumerate(blocks):
        _ssd_decays(dt_raw[rows], dtb_ref[...], alog_ref[...], heads, causal, cs_ref.at[b],
                    rows_ref.at[b])
        for g in range(SSM_GROUPS):
            group = _ssd_group(act_ref, rows, g, d_inner, n_state)
            for pair in range(pairs_per_group):
                _ssd_pair(act_ref, rows, g * pairs_per_group + pair, group, causal, first_half,
                          dsk_ref, st_ref, y_ref, cs_ref.at[b], rows_ref.at[b])
                pairs_done += 1
                while side_done * n_pairs < pairs_done * len(side):
                    side[side_done]()
                    side_done += 1

    gw = d_inner // SSM_GROUPS
    for g in range(SSM_GROUPS):
        cols = slice(g * gw, (g + 1) * gw)
        yg = y_ref[:, cols] * zs_ref[:, cols].astype(F32)
        yn_ref[:, cols] = _rms(yg, snw_ref[:, cols]).astype(BF16)

    for lo in range(0, d, COL_TILE):
        cols = slice(lo, lo + COL_TILE)
        gcols = slice(d + lo, d + lo + COL_TILE)
        merged = (g_ref[:, cols].astype(F32) * _dot(yn_ref[...], wos_ref[:, cols])
                  + g_ref[:, gcols].astype(F32) * _dot(lr_ref[...], wol_ref[:, cols]))
        mg_ref[:, cols] = merged.astype(BF16)
    mixed = _dot(mg_ref[...], wo_ref[...])
    out_ref[...] = x_ref[...] + _unpermute_rows(slab_ref, mixed)


def _mixer(x2, batch, seq, weights):
    t, d = x2.shape
    tm = STEP_TOKENS
    nc = seq // tm
    (nw1, wg, bg, wz, wxbc, wdt, wlx, wly, cw, cb, dtb, alog, dsk, snw,
     lcw, lcb, wri, br, bi, lam, wos, wol, wo) = weights
    d_inner = wz.shape[1]
    conv_dim = wxbc.shape[1]
    n_state = (conv_dim - d_inner) // (2 * SSM_GROUPS)
    heads = d_inner // SSM_HEAD_DIM
    width = wlx.shape[1]
    row = lambda b, c: (b * nc + c, 0)
    return pl.pallas_call(
        _mixer_kernel,
        grid=(batch, nc),
        in_specs=[pl.BlockSpec((tm, d), row)] + [_resident(w.shape) for w in weights],
        out_specs=pl.BlockSpec((tm, d), row),
        out_shape=jax.ShapeDtypeStruct((t, d), F32),
        scratch_shapes=[
            pltpu.VMEM((d // LANES, tm, LANES), F32),
            pltpu.VMEM((tm, d), BF16),
            pltpu.VMEM((tm, conv_dim), F32),
            pltpu.VMEM((tm, d_inner), F32),
            pltpu.VMEM((tm, d_inner), BF16),
            pltpu.VMEM((tm, width), BF16),
            pltpu.VMEM((tm, d), BF16),
            pltpu.VMEM((tm, width), F32),
            pltpu.VMEM((tm, width), BF16),
            pltpu.VMEM((tm, d_inner), BF16),
            pltpu.VMEM((tm, 2 * d), BF16),
            pltpu.VMEM((cw.shape[0] - 1, conv_dim), F32),
            pltpu.VMEM((d_inner // LANES, n_state, LANES), F32),
            pltpu.VMEM((STEP_BLOCKS, PERM_BLOCK, LANES), F32),
            pltpu.VMEM((STEP_BLOCKS, 2, heads, PERM_BLOCK), F32),
            pltpu.VMEM((lcw.shape[0] - 1, width), F32),
            pltpu.VMEM((1, width), F32),
            pltpu.VMEM((1, width), F32),
        ],
        compiler_params=pltpu.CompilerParams(dimension_semantics=("arbitrary", "arbitrary"),
                                             vmem_limit_bytes=VMEM_LIMIT_BYTES),
        name="mixer",
    )(x2, *weights)


def _ffn_kernel(h_ref, n2_ref, wfg_ref, wfu_ref, wfo_ref, nf_ref, out_ref, *, final_norm):
    h = h_ref[...]
    hn = _rms(h, n2_ref[...]).astype(BF16)
    hidden = wfg_ref.shape[1]
    acc = h
    for lo in range(0, hidden, COL_TILE):
        hi = min(lo + COL_TILE, hidden)
        act = (_silu(_dot(hn, wfg_ref[:, lo:hi])) * _dot(hn, wfu_ref[:, lo:hi])).astype(BF16)
        acc = acc + _dot(act, wfo_ref[lo:hi, :])
    if final_norm:
        acc = _rms(acc, nf_ref[...])
    out_ref[...] = acc


def _ffn(h2, n2, wfg, wfu, wfo, nf, final_norm):
    t, d = h2.shape
    tm = FFN_ROWS
    row = lambda i: (i, 0)
    weights = (n2, wfg, wfu, wfo, nf)
    return pl.pallas_call(
        functools.partial(_ffn_kernel, final_norm=final_norm),
        grid=(t // tm,),
        in_specs=[pl.BlockSpec((tm, d), row)] + [_resident(w.shape) for w in weights],
        out_specs=pl.BlockSpec((tm, d), row),
        out_shape=jax.ShapeDtypeStruct((t, d), F32),
        compiler_params=pltpu.CompilerParams(dimension_semantics=("arbitrary",),
                                             vmem_limit_bytes=VMEM_LIMIT_BYTES),
        name="ffn",
    )(h2, *weights)


def _pad_lanes(a):
    pad = (-a.shape[-1]) % LANES
    return jnp.pad(a, [(0, 0)] * (a.ndim - 1) + [(0, pad)])


def kernel(x, norm1_w, w_in, b_branch_gate, ssm_conv_w, ssm_conv_b, ssm_dt_bias, ssm_a_log, ssm_d, ssm_norm_w, w_out_ssm, lru_conv_w, lru_conv_b, lru_w_r, lru_b_r, lru_w_i, lru_b_i, lru_lambda, w_out_lru, w_out, norm2_w, w_ffn_in, w_ffn_out, norm_f_w):
    batch, seq, d = x.shape
    depth = norm1_w.shape[0]
    d_inner = w_out_ssm.shape[1]
    conv_dim = ssm_conv_w.shape[2]
    heads = ssm_dt_bias.shape[1]
    width = w_out_lru.shape[1]
    hidden = w_ffn_out.shape[1]
    n_gates = b_branch_gate.shape[1]
    assert seq % STEP_TOKENS == 0 and (batch * seq) % FFN_ROWS == 0
    assert d_inner == heads * SSM_HEAD_DIM and n_gates == 2 * d
    assert d % COL_TILE == 0 and conv_dim % COL_TILE == 0 and width % (2 * LANES) == 0
    assert (d_inner // SSM_GROUPS) % LANES == 0 and d_inner % SIDE_TILE == 0

    o_z = n_gates
    o_xbc = o_z + d_inner
    o_dt = o_xbc + conv_dim
    o_lx = o_dt + heads
    o_ly = o_lx + width

    h2 = x.reshape(batch * seq, d)
    for l in range(depth):
        wi = w_in[l].astype(BF16)
        row2 = lambda a: a[l].reshape(1, -1)
        weights = (
            row2(norm1_w), wi[:, :o_z], row2(b_branch_gate), wi[:, o_z:o_xbc], wi[:, o_xbc:o_dt],
            _pad_lanes(wi[:, o_dt:o_lx]), wi[:, o_lx:o_ly], wi[:, o_ly:],
            ssm_conv_w[l], row2(ssm_conv_b), _pad_lanes(row2(ssm_dt_bias)), _pad_lanes(row2(ssm_a_log)),
            jnp.repeat(ssm_d[l], SSM_HEAD_DIM).reshape(1, -1), row2(ssm_norm_w),
            lru_conv_w[l], row2(lru_conv_b),
            jnp.concatenate([lru_w_r[l], lru_w_i[l]], axis=-1).astype(BF16),
            row2(lru_b_r), row2(lru_b_i), row2(lru_lambda),
            w_out_ssm[l].astype(BF16), w_out_lru[l].astype(BF16), w_out[l].astype(BF16))
        h2 = _mixer(h2, batch, seq, weights)
        wf = w_ffn_in[l].astype(BF16)
        h2 = _ffn(h2, row2(norm2_w), wf[:, :hidden], wf[:, hidden:], w_ffn_out[l].astype(BF16),
                  norm_f_w.reshape(1, -1), final_norm=(l == depth - 1))
    return h2.reshape(batch, seq, d)
```

```python
import functools

import jax
import jax.numpy as jnp
from jax import lax
from jax.experimental import pallas as pl
from jax.experimental.pallas import tpu as pltpu

F32 = jnp.float32
BF16 = jnp.bfloat16

SSM_GROUPS = 4
SSM_HEAD_DIM = 64
LRU_C = 8.0
RMS_EPS = 1e-6
LOG2_E = 1.4426950408889634

LANES = 128
SUBLANES = 8
VMEM_LIMIT_BYTES = 56 * 1024 * 1024

PERM_BLOCK = 128
GROUP = PERM_BLOCK // SUBLANES
STEP_TOKENS = 256
STEP_BLOCKS = STEP_TOKENS // PERM_BLOCK
FFN_ROWS = 512
COL_TILE = 512
MIX_TILE = 256
SIDE_TILE = 256


def _resident(shape):
    nd = len(shape)
    return pl.BlockSpec(shape, lambda *_: (0,) * nd, pipeline_mode=pl.Buffered(1))


def _softplus(x):
    return jnp.maximum(x, 0.0) + jnp.log1p(jnp.exp(-jnp.abs(x)))


def _sigmoid(x):
    return 0.5 * jnp.tanh(0.5 * x) + 0.5


def _silu(x):
    hx = 0.5 * x
    return hx * jnp.tanh(hx) + hx


def _gelu_tanh(x):
    c0 = 0.7978845608028654
    hx = 0.5 * x
    return hx * jnp.tanh(x * (c0 + (c0 * 0.044715) * (x * x))) + hx


def _sqrt_unit(x):
    return jnp.where(x > 0.0, x * lax.rsqrt(x), 0.0)


def _rms(x, w):
    ms = jnp.mean(x * x, axis=-1, keepdims=True)
    return x * lax.rsqrt(ms + RMS_EPS) * w


def _split3_bf16(v):
    hi = v.astype(BF16)
    r = v - hi.astype(F32)
    mid = r.astype(BF16)
    lo = (r - mid.astype(F32)).astype(BF16)
    return hi, mid, lo


def _dot(a, b):
    return jnp.dot(a, b, preferred_element_type=F32)


def _strided_rows(blk, r):
    return pl.ds(blk * PERM_BLOCK + r, GROUP, stride=SUBLANES)


def _permute_rows(slab_ref, value):
    rows, cols = value.shape
    out = []
    for s in range(cols // LANES):
        slab_ref[s] = value[:, s * LANES:(s + 1) * LANES]
        out.append(jnp.concatenate(
            [slab_ref[s, _strided_rows(blk, r), :]
             for blk in range(rows // PERM_BLOCK) for r in range(SUBLANES)], axis=0))
    return jnp.concatenate(out, axis=1)


def _unpermute_rows(slab_ref, value):
    rows, cols = value.shape
    out = []
    for s in range(cols // LANES):
        for blk in range(rows // PERM_BLOCK):
            for r in range(SUBLANES):
                p = blk * PERM_BLOCK + r * GROUP
                slab_ref[s, _strided_rows(blk, r), :] = value[p:p + GROUP, s * LANES:(s + 1) * LANES]
        out.append(slab_ref[s])
    return jnp.concatenate(out, axis=1)


def _shift_rows(v, s, fill):
    n = v.shape[0]
    if s % SUBLANES == 0:
        head = jnp.broadcast_to(jnp.asarray(fill, v.dtype), (s, v.shape[1]))
        return jnp.concatenate([head, v[:n - s]], axis=0)
    t = lax.broadcasted_iota(jnp.int32, v.shape, 0)
    return jnp.where(t >= s, pltpu.roll(v, s, axis=0), fill)


def _perm_conv(cur, tail, cw, cb):
    k_width = cw.shape[0]
    slabs = [cur[GROUP * r:GROUP * (r + 1)] for r in range(SUBLANES)]
    prev_group = {}
    for d in range(1, k_width):
        q = SUBLANES - d
        prev_group[q] = _shift_rows(slabs[q], 1, tail[k_width - 1 - d:k_width - d, :])
    outs = []
    for r in range(SUBLANES):
        acc = cb
        for k in range(k_width):
            d = k_width - 1 - k
            src = slabs[r - d] if r >= d else prev_group[r - d + SUBLANES]
            acc = acc + cw[k:k + 1, :] * src
        outs.append(acc)
    return jnp.concatenate(outs, axis=0)


def _conv_tail(cur, k_width):
    rows = [GROUP * (SUBLANES - d) + GROUP - 1 for d in range(k_width - 1, 0, -1)]
    return jnp.concatenate([cur[p:p + 1] for p in rows], axis=0)


def _perm_scan(a, b, h_prev):
    pa = [a[0:GROUP]]
    pb = [b[0:GROUP]]
    for r in range(1, SUBLANES):
        ar = a[GROUP * r:GROUP * (r + 1)]
        pa.append(ar * pa[-1])
        pb.append(ar * pb[-1] + b[GROUP * r:GROUP * (r + 1)])
    ga, gb = pa[-1], pb[-1]
    s = 1
    while s < GROUP:
        gb = ga * _shift_rows(gb, s, 0.0) + gb
        ga = ga * _shift_rows(ga, s, 1.0)
        s *= 2
    h_end = ga * h_prev + gb
    h_in = _shift_rows(h_end, 1, h_prev)
    h = jnp.concatenate([pa[r] * h_in + pb[r] for r in range(SUBLANES)], axis=0)
    return h, h_end[GROUP - 1:GROUP]


def _causal_mask():
    lc = PERM_BLOCK
    ri = lax.broadcasted_iota(jnp.int32, (lc, lc), 0)
    ci = lax.broadcasted_iota(jnp.int32, (lc, lc), 1)
    time_of = lambda p: SUBLANES * (p & (GROUP - 1)) + (p >> (GROUP.bit_length() - 1))
    return time_of(ri) >= time_of(ci)


def _ssd_decays(dt_raw, dtb, a_log, heads, causal, cs_ref, rows_ref):
    lc = PERM_BLOCK
    dt = _softplus(dt_raw + dtb)
    d_a = dt * (-jnp.exp(a_log))
    tri = causal.astype(BF16)
    hi, mid, lo = _split3_bf16(d_a)
    cs = (_dot(tri, hi) + _dot(tri, mid) + _dot(tri, lo)) * LOG2_E
    cs_ref[...] = cs
    cs_t = cs.T[:heads]
    dt_t = dt.T[:heads]
    rows_ref[0] = cs_t - jnp.log2(dt_t)
    rows_ref[1] = dt_t * jnp.exp2(cs_t[:, lc - 1:lc] - cs_t)


def _ssd_group(act_ref, rows, g, d_inner, n_state):
    b_off = d_inner + g * n_state
    c_off = d_inner + SSM_GROUPS * n_state + g * n_state
    bm = act_ref[rows, b_off:b_off + n_state]
    cm = act_ref[rows, c_off:c_off + n_state]
    cb = lax.dot_general(cm.astype(BF16), bm.astype(BF16), (((1,), (1,)), ((), ())),
                         preferred_element_type=F32)
    return cm, cb, bm.T


def _ssd_pair(act_ref, rows, j, group, causal, first_half, dsk_ref, st_ref, y_ref, cs_ref, rows_ref):
    lc = PERM_BLOCK
    cm, cb, bm_t = group
    cols = slice(j * LANES, (j + 1) * LANES)
    xs = act_ref[rows, cols]
    s_old = st_ref[j]
    xs_b = xs.astype(BF16)
    rhs = jnp.concatenate([xs_b, s_old.astype(BF16)], axis=0)
    ys, upds, tots = [], [], []
    for k in range(2):
        h = 2 * j + k
        col = jnp.broadcast_to(cs_ref[:, h:h + 1], (lc, lc))
        row = rows_ref[0, h:h + 1, :]
        seg = jnp.exp2(jnp.where(causal, col - row, -jnp.inf))
        g_mat = (cb * seg).astype(BF16)
        c_mat = (cm * jnp.exp2(col)).astype(BF16)
        ys.append(_dot(jnp.concatenate([g_mat, c_mat], axis=1), rhs))
        bw = (bm_t * rows_ref[1, h:h + 1, :]).astype(BF16)
        upds.append(_dot(bw, xs_b))
        tots.append(jnp.exp2(col[lc - 1:lc, :]))
    y_ref[rows, cols] = xs * dsk_ref[:, cols] + jnp.where(first_half, ys[0], ys[1])
    st_ref[j] = (s_old * jnp.where(first_half, tots[0], tots[1])
                 + jnp.where(first_half, upds[0], upds[1]))


def _mixer_kernel(x_ref, nw1_ref, wm_ref, bg_ref, wdt_ref, wlx_ref, wly_ref,
                  cw_ref, cb_ref, dtb_ref, alog_ref, dsk_ref, snw_ref,
                  lcw_ref, lcb_ref, wri_ref, br_ref, bi_ref, lam_ref,
                  wos_ref, wol_ref, wo_ref,
                  out_ref,
                  slab_ref, hn_ref, act_ref, y_ref, yn_ref, lr_ref, mg_ref, lx_ref, gy_ref,
                  zs_ref, g_ref, ctail_ref, st_ref, cs_ref, rows_ref, ltail_ref, h_ref, sp_ref):
    d = x_ref.shape[1]
    d_inner = snw_ref.shape[1]
    conv_dim = cw_ref.shape[1]
    o_z = bg_ref.shape[1]
    o_xbc = o_z + d_inner
    n_state = (conv_dim - d_inner) // (2 * SSM_GROUPS)
    heads = d_inner // SSM_HEAD_DIM
    pairs_per_group = heads // SSM_GROUPS // 2
    width = wlx_ref.shape[1]
    k_ssm = cw_ref.shape[0]
    k_lru = lcw_ref.shape[0]
    blocks = [slice(b * PERM_BLOCK, (b + 1) * PERM_BLOCK) for b in range(STEP_BLOCKS)]

    @pl.when(pl.program_id(1) == 0)
    def _():
        ctail_ref[...] = jnp.zeros(ctail_ref.shape, F32)
        st_ref[...] = jnp.zeros(st_ref.shape, F32)
        ltail_ref[...] = jnp.zeros(ltail_ref.shape, F32)
        h_ref[...] = jnp.zeros(h_ref.shape, F32)
        sp_ref[...] = _softplus(-lam_ref[...])

    hn_ref[...] = _permute_rows(slab_ref, _rms(x_ref[...], nw1_ref[...])).astype(BF16)

    def project_lru(lo):
        pcols = slice(lo, lo + 2 * LANES)
        lx_ref[:, pcols] = _dot(hn_ref[...], wlx_ref[:, pcols])
        gy_ref[:, pcols] = _gelu_tanh(_dot(hn_ref[...], wly_ref[:, pcols]))

    def conv_xbc(lo):
        cols = slice(lo, lo + MIX_TILE)
        proj = _dot(hn_ref[...], wm_ref[:, o_xbc + lo:o_xbc + lo + MIX_TILE])
        for rows in blocks:
            cur = proj[rows]
            act_ref[rows, cols] = _silu(_perm_conv(cur, ctail_ref[:, cols], cw_ref[:, cols], cb_ref[:, cols]))
            ctail_ref[:, cols] = _conv_tail(cur, k_ssm)

    def lru_block(j):
        cols = slice(j * LANES, (j + 1) * LANES)
        u_blocks = []
        for rows in blocks:
            cur = lx_ref[rows, cols]
            u_blocks.append(_perm_conv(cur, ltail_ref[:, cols], lcw_ref[:, cols], lcb_ref[:, cols]))
            ltail_ref[:, cols] = _conv_tail(cur, k_lru)
        u = jnp.concatenate(u_blocks, axis=0)
        ri = _dot(u.astype(BF16), wri_ref[j])
        r_gate = _sigmoid(ri[:, :LANES] + br_ref[:, cols])
        i_gate = _sigmoid(ri[:, LANES:] + bi_ref[:, cols])
        log_a = (-LRU_C) * r_gate * sp_ref[:, cols]
        a = jnp.exp(log_a)
        b_in = _sqrt_unit(1.0 - jnp.exp(2.0 * log_a)) * (i_gate * u)
        for rows in blocks:
            h, h_last = _perm_scan(a[rows], b_in[rows], h_ref[:, cols])
            h_ref[:, cols] = h_last
            lr_ref[rows, cols] = (h * gy_ref[rows, cols]).astype(BF16)

    lru_slices = list(range(0, width, 2 * LANES))
    xbc_slices = list(range(0, conv_dim, MIX_TILE))
    n_lru = width // LANES
    project_lru(lru_slices[0])
    for i in range(max(n_lru, len(xbc_slices))):
        if i % 2 == 0 and i // 2 + 1 < len(lru_slices):
            project_lru(lru_slices[i // 2 + 1])
        if i < len(xbc_slices):
            conv_xbc(xbc_slices[i])
        if i < n_lru:
            lru_block(i)

    def side_z(lo):
        cols = slice(lo, lo + SIDE_TILE)
        zs_ref[:, cols] = _silu(_dot(hn_ref[...], wm_ref[:, o_z + lo:o_z + lo + SIDE_TILE]))

    def side_g(lo):
        cols = slice(lo, lo + SIDE_TILE)
        g_ref[:, cols] = _sigmoid(_dot(hn_ref[...], wm_ref[:, cols]) + bg_ref[:, cols])

    side = ([functools.partial(side_z, lo) for lo in range(0, d_inner, SIDE_TILE)]
            + [functools.partial(side_g, lo) for lo in range(0, 2 * d, SIDE_TILE)])
    n_pairs = STEP_BLOCKS * SSM_GROUPS * pairs_per_group
    dt_raw = _dot(hn_ref[...], wdt_ref[...])
    pairs_done = 0
    side_done = 0
    causal = _causal_mask()
    first_half = lax.broadcasted_iota(jnp.int32, (1, LANES), 1) < SSM_HEAD_DIM
    for b, rows in enumerate(blocks):
        _ssd_decays(dt_raw[rows], dtb_ref[...], alog_ref[...], heads, causal, cs_ref.at[b],
                    rows_ref.at[b])
        for g in range(SSM_GROUPS):
            group = _ssd_group(act_ref, rows, g, d_inner, n_state)
            for pair in range(pairs_per_group):
                _ssd_pair(act_ref, rows, g * pairs_per_group + pair, group, causal, first_half,
                          dsk_ref, st_ref, y_ref, cs_ref.at[b], rows_ref.at[b])
                pairs_done += 1
                while side_done * n_pairs < pairs_done * len(side):
                    side[side_done]()
                    side_done += 1

    gw = d_inner // SSM_GROUPS
    for g in range(SSM_GROUPS):
        cols = slice(g * gw, (g + 1) * gw)
        yg = y_ref[:, cols] * zs_ref[:, cols]
        yn_ref[:, cols] = _rms(yg, snw_ref[:, cols]).astype(BF16)

    for lo in range(0, d, MIX_TILE):
        cols = slice(lo, lo + MIX_TILE)
        gcols = slice(d + lo, d + lo + MIX_TILE)
        merged = (g_ref[:, cols] * _dot(yn_ref[...], wos_ref[:, cols])
                  + g_ref[:, gcols] * _dot(lr_ref[...], wol_ref[:, cols]))
        mg_ref[:, cols] = merged.astype(BF16)
    mixed = _dot(mg_ref[...], wo_ref[...])
    out_ref[...] = x_ref[...] + _unpermute_rows(slab_ref, mixed)


def _mixer(x2, batch, seq, weights):
    t, d = x2.shape
    tm = STEP_TOKENS
    nc = seq // tm
    (nw1, wm, bg, wdt, wlx, wly, cw, cb, dtb, alog, dsk, snw,
     lcw, lcb, wri, br, bi, lam, wos, wol, wo) = weights
    d_inner = snw.shape[1]
    conv_dim = cw.shape[1]
    n_state = (conv_dim - d_inner) // (2 * SSM_GROUPS)
    heads = d_inner // SSM_HEAD_DIM
    width = wlx.shape[1]
    row = lambda b, c: (b * nc + c, 0)
    return pl.pallas_call(
        _mixer_kernel,
        grid=(batch, nc),
        in_specs=[pl.BlockSpec((tm, d), row)] + [_resident(w.shape) for w in weights],
        out_specs=pl.BlockSpec((tm, d), row),
        out_shape=jax.ShapeDtypeStruct((t, d), F32),
        scratch_shapes=[
            pltpu.VMEM((d // LANES, tm, LANES), F32),
            pltpu.VMEM((tm, d), BF16),
            pltpu.VMEM((tm, conv_dim), F32),
            pltpu.VMEM((tm, d_inner), F32),
            pltpu.VMEM((tm, d_inner), BF16),
            pltpu.VMEM((tm, width), BF16),
            pltpu.VMEM((tm, d), BF16),
            pltpu.VMEM((tm, width), F32),
            pltpu.VMEM((tm, width), F32),
            pltpu.VMEM((tm, d_inner), F32),
            pltpu.VMEM((tm, 2 * d), F32),
            pltpu.VMEM((cw.shape[0] - 1, conv_dim), F32),
            pltpu.VMEM((d_inner // LANES, n_state, LANES), F32),
            pltpu.VMEM((STEP_BLOCKS, PERM_BLOCK, LANES), F32),
            pltpu.VMEM((STEP_BLOCKS, 2, heads, PERM_BLOCK), F32),
            pltpu.VMEM((lcw.shape[0] - 1, width), F32),
            pltpu.VMEM((1, width), F32),
            pltpu.VMEM((1, width), F32),
        ],
        compiler_params=pltpu.CompilerParams(dimension_semantics=("arbitrary", "arbitrary"),
                                             vmem_limit_bytes=VMEM_LIMIT_BYTES),
        name="mixer",
    )(x2, *weights)


def _ffn_kernel(h_ref, n2_ref, wfi_ref, wfo_ref, nf_ref, out_ref, *, final_norm):
    h = h_ref[...]
    hn = _rms(h, n2_ref[...]).astype(BF16)
    hidden = wfo_ref.shape[0]
    acc = h
    for lo in range(0, hidden, COL_TILE):
        hi = min(lo + COL_TILE, hidden)
        act = (_silu(_dot(hn, wfi_ref[:, lo:hi]))
               * _dot(hn, wfi_ref[:, hidden + lo:hidden + hi])).astype(BF16)
        acc = acc + _dot(act, wfo_ref[lo:hi, :])
    if final_norm:
        acc = _rms(acc, nf_ref[...])
    out_ref[...] = acc


def _ffn(h2, n2, wfi, wfo, nf, final_norm):
    t, d = h2.shape
    tm = FFN_ROWS
    row = lambda i: (i, 0)
    weights = (n2, wfi, wfo, nf)
    return pl.pallas_call(
        functools.partial(_ffn_kernel, final_norm=final_norm),
        grid=(t // tm,),
        in_specs=[pl.BlockSpec((tm, d), row)] + [_resident(w.shape) for w in weights],
        out_specs=pl.BlockSpec((tm, d), row),
        out_shape=jax.ShapeDtypeStruct((t, d), F32),
        compiler_params=pltpu.CompilerParams(dimension_semantics=("arbitrary",),
                                             vmem_limit_bytes=VMEM_LIMIT_BYTES),
        name="ffn",
    )(h2, *weights)


def _pad_lanes(a):
    pad = (-a.shape[-1]) % LANES
    return jnp.pad(a, [(0, 0)] * (a.ndim - 1) + [(0, pad)])


def kernel(x, norm1_w, w_in, b_branch_gate, ssm_conv_w, ssm_conv_b, ssm_dt_bias, ssm_a_log, ssm_d, ssm_norm_w, w_out_ssm, lru_conv_w, lru_conv_b, lru_w_r, lru_b_r, lru_w_i, lru_b_i, lru_lambda, w_out_lru, w_out, norm2_w, w_ffn_in, w_ffn_out, norm_f_w):
    batch, seq, d = x.shape
    depth = norm1_w.shape[0]
    d_inner = w_out_ssm.shape[1]
    conv_dim = ssm_conv_w.shape[2]
    heads = ssm_dt_bias.shape[1]
    width = w_out_lru.shape[1]
    hidden = w_ffn_out.shape[1]
    n_gates = b_branch_gate.shape[1]
    assert seq % STEP_TOKENS == 0 and (batch * seq) % FFN_ROWS == 0
    assert d_inner == heads * SSM_HEAD_DIM and n_gates == 2 * d
    assert d % MIX_TILE == 0 and conv_dim % MIX_TILE == 0 and width % (2 * LANES) == 0
    assert (d_inner // SSM_GROUPS) % LANES == 0 and d_inner % SIDE_TILE == 0 and hidden % LANES == 0

    o_z = n_gates
    o_xbc = o_z + d_inner
    o_dt = o_xbc + conv_dim
    o_lx = o_dt + heads
    o_ly = o_lx + width

    h2 = x.reshape(batch * seq, d)
    for l in range(depth):
        wi = w_in[l]
        row2 = lambda a: a[l].reshape(1, -1)
        weights = (
            row2(norm1_w), wi[:, :o_dt].astype(BF16), row2(b_branch_gate),
            _pad_lanes(wi[:, o_dt:o_lx].astype(BF16)), wi[:, o_lx:o_ly].astype(BF16),
            wi[:, o_ly:].astype(BF16),
            ssm_conv_w[l], row2(ssm_conv_b), _pad_lanes(row2(ssm_dt_bias)), _pad_lanes(row2(ssm_a_log)),
            jnp.repeat(ssm_d[l], SSM_HEAD_DIM).reshape(1, -1), row2(ssm_norm_w),
            lru_conv_w[l], row2(lru_conv_b),
            jnp.concatenate([lru_w_r[l], lru_w_i[l]], axis=-1).astype(BF16),
            row2(lru_b_r), row2(lru_b_i), row2(lru_lambda),
            w_out_ssm[l].astype(BF16), w_out_lru[l].astype(BF16), w_out[l].astype(BF16))
        h2 = _mixer(h2, batch, seq, weights)
        h2 = _ffn(h2, row2(norm2_w), w_ffn_in[l].astype(BF16), w_ffn_out[l].astype(BF16),
                  norm_f_w.reshape(1, -1), final_norm=(l == depth - 1))
    return h2.reshape(batch, seq, d)
```

```python
import functools

import jax
import jax.numpy as jnp
from jax import lax
from jax.experimental import pallas as pl
from jax.experimental.pallas import tpu as pltpu

F32 = jnp.float32
BF16 = jnp.bfloat16

SSM_GROUPS = 4
SSM_HEAD_DIM = 64
LRU_C = 8.0
RMS_EPS = 1e-6
LOG2_E = 1.4426950408889634

LANES = 128
SUBLANES = 8
VMEM_LIMIT_BYTES = 56 * 1024 * 1024

PERM_BLOCK = 128
GROUP = PERM_BLOCK // SUBLANES
STEP_TOKENS = 256
STEP_BLOCKS = STEP_TOKENS // PERM_BLOCK
FFN_ROWS = 512
COL_TILE = 512
MIX_TILE = 256
SIDE_TILE = 256


def _resident(shape):
    nd = len(shape)
    return pl.BlockSpec(shape, lambda *_: (0,) * nd, pipeline_mode=pl.Buffered(1))


def _softplus(x):
    return jnp.maximum(x, 0.0) + jnp.log1p(jnp.exp(-jnp.abs(x)))


def _sigmoid(x):
    return 0.5 * jnp.tanh(0.5 * x) + 0.5


def _silu(x):
    hx = 0.5 * x
    return hx * jnp.tanh(hx) + hx


def _gelu_tanh(x):
    c0 = 0.7978845608028654
    hx = 0.5 * x
    return hx * jnp.tanh(x * (c0 + (c0 * 0.044715) * (x * x))) + hx


def _sqrt_unit(x):
    return jnp.where(x > 0.0, x * lax.rsqrt(x), 0.0)


def _rms(x, w):
    ms = jnp.mean(x * x, axis=-1, keepdims=True)
    return x * lax.rsqrt(ms + RMS_EPS) * w


def _split3_bf16(v):
    hi = v.astype(BF16)
    r = v - hi.astype(F32)
    mid = r.astype(BF16)
    lo = (r - mid.astype(F32)).astype(BF16)
    return hi, mid, lo


def _dot(a, b):
    return jnp.dot(a, b, preferred_element_type=F32)


def _strided_rows(blk, r):
    return pl.ds(blk * PERM_BLOCK + r, GROUP, stride=SUBLANES)


def _permute_rows(slab_ref, value):
    rows, cols = value.shape
    out = []
    for s in range(cols // LANES):
        slab_ref[s] = value[:, s * LANES:(s + 1) * LANES]
        out.append(jnp.concatenate(
            [slab_ref[s, _strided_rows(blk, r), :]
             for blk in range(rows // PERM_BLOCK) for r in range(SUBLANES)], axis=0))
    return jnp.concatenate(out, axis=1)


def _unpermute_rows(slab_ref, value):
    rows, cols = value.shape
    out = []
    for s in range(cols // LANES):
        for blk in range(rows // PERM_BLOCK):
            for r in range(SUBLANES):
                p = blk * PERM_BLOCK + r * GROUP
                slab_ref[s, _strided_rows(blk, r), :] = value[p:p + GROUP, s * LANES:(s + 1) * LANES]
        out.append(slab_ref[s])
    return jnp.concatenate(out, axis=1)


def _shift_rows(v, s, fill):
    n = v.shape[0]
    if s % SUBLANES == 0:
        head = jnp.broadcast_to(jnp.asarray(fill, v.dtype), (s, v.shape[1]))
        return jnp.concatenate([head, v[:n - s]], axis=0)
    t = lax.broadcasted_iota(jnp.int32, v.shape, 0)
    return jnp.where(t >= s, pltpu.roll(v, s, axis=0), fill)


def _perm_conv(cur, tail, cw, cb):
    k_width = cw.shape[0]
    slabs = [cur[GROUP * r:GROUP * (r + 1)] for r in range(SUBLANES)]
    prev_group = {}
    for d in range(1, k_width):
        q = SUBLANES - d
        prev_group[q] = _shift_rows(slabs[q], 1, tail[k_width - 1 - d:k_width - d, :])
    outs = []
    for r in range(SUBLANES):
        acc = cb
        for k in range(k_width):
            d = k_width - 1 - k
            src = slabs[r - d] if r >= d else prev_group[r - d + SUBLANES]
            acc = acc + cw[k:k + 1, :] * src
        outs.append(acc)
    return jnp.concatenate(outs, axis=0)


def _conv_tail(cur, k_width):
    rows = [GROUP * (SUBLANES - d) + GROUP - 1 for d in range(k_width - 1, 0, -1)]
    return jnp.concatenate([cur[p:p + 1] for p in rows], axis=0)


def _perm_scan(a, b, h_prev):
    pa = [a[0:GROUP]]
    pb = [b[0:GROUP]]
    for r in range(1, SUBLANES):
        ar = a[GROUP * r:GROUP * (r + 1)]
        pa.append(ar * pa[-1])
        pb.append(ar * pb[-1] + b[GROUP * r:GROUP * (r + 1)])
    ga, gb = pa[-1], pb[-1]
    s = 1
    while s < GROUP:
        gb = ga * _shift_rows(gb, s, 0.0) + gb
        ga = ga * _shift_rows(ga, s, 1.0)
        s *= 2
    h_end = ga * h_prev + gb
    h_in = _shift_rows(h_end, 1, h_prev)
    h = jnp.concatenate([pa[r] * h_in + pb[r] for r in range(SUBLANES)], axis=0)
    return h, h_end[GROUP - 1:GROUP]


def _causal_mask():
    lc = PERM_BLOCK
    ri = lax.broadcasted_iota(jnp.int32, (lc, lc), 0)
    ci = lax.broadcasted_iota(jnp.int32, (lc, lc), 1)
    time_of = lambda p: SUBLANES * (p & (GROUP - 1)) + (p >> (GROUP.bit_length() - 1))
    return time_of(ri) >= time_of(ci)


def _ssd_decays(dt_raw, dtb, a_log, heads, causal, cs_ref, rows_ref):
    lc = PERM_BLOCK
    dt = _softplus(dt_raw + dtb)
    d_a = dt * (-jnp.exp(a_log))
    tri = causal.astype(BF16)
    hi, mid, lo = _split3_bf16(d_a)
    cs = (_dot(tri, hi) + _dot(tri, mid) + _dot(tri, lo)) * LOG2_E
    cs_ref[...] = cs
    cs_t = cs.T[:heads]
    dt_t = dt.T[:heads]
    rows_ref[0] = cs_t - jnp.log2(dt_t)
    rows_ref[1] = dt_t * jnp.exp2(cs_t[:, lc - 1:lc] - cs_t)


def _ssd_group(act_ref, rows, g, d_inner, n_state):
    b_off = d_inner + g * n_state
    c_off = d_inner + SSM_GROUPS * n_state + g * n_state
    bm = act_ref[rows, b_off:b_off + n_state]
    cm = act_ref[rows, c_off:c_off + n_state]
    cb = lax.dot_general(cm.astype(BF16), bm.astype(BF16), (((1,), (1,)), ((), ())),
                         preferred_element_type=F32)
    return cm, cb, bm.T


def _ssd_pair(act_ref, rows, j, group, causal, first_half, dsk_ref, st_ref, y_ref, cs_ref, rows_ref):
    lc = PERM_BLOCK
    cm, cb, bm_t = group
    cols = slice(j * LANES, (j + 1) * LANES)
    xs = act_ref[rows, cols]
    s_old = st_ref[j]
    xs_b = xs.astype(BF16)
    rhs = jnp.concatenate([xs_b, s_old.astype(BF16)], axis=0)
    ys, upds, tots = [], [], []
    for k in range(2):
        h = 2 * j + k
        col = jnp.broadcast_to(cs_ref[:, h:h + 1], (lc, lc))
        row = rows_ref[0, h:h + 1, :]
        seg = jnp.exp2(jnp.where(causal, col - row, -jnp.inf))
        g_mat = (cb * seg).astype(BF16)
        c_mat = (cm * jnp.exp2(col)).astype(BF16)
        ys.append(_dot(jnp.concatenate([g_mat, c_mat], axis=1), rhs))
        bw = (bm_t * rows_ref[1, h:h + 1, :]).astype(BF16)
        upds.append(_dot(bw, xs_b))
        tots.append(jnp.exp2(col[lc - 1:lc, :]))
    y_ref[rows, cols] = xs * dsk_ref[:, cols] + jnp.where(first_half, ys[0], ys[1])
    st_ref[j] = (s_old * jnp.where(first_half, tots[0], tots[1])
                 + jnp.where(first_half, upds[0], upds[1]))


def _mixer_kernel(x_ref, nw1_ref, wm_ref, bg_ref, wdt_ref, wlx_ref, wly_ref,
                  cw_ref, cb_ref, dtb_ref, alog_ref, dsk_ref, snw_ref,
                  lcw_ref, lcb_ref, wri_ref, br_ref, bi_ref, lam_ref,
                  wos_ref, wol_ref, wo_ref,
                  out_ref,
                  slab_ref, hn_ref, act_ref, y_ref, yn_ref, lr_ref, mg_ref, lx_ref, gy_ref,
                  zs_ref, g_ref, ctail_ref, st_ref, cs_ref, rows_ref, ltail_ref, h_ref, sp_ref):
    d = x_ref.shape[1]
    d_inner = snw_ref.shape[1]
    conv_dim = cw_ref.shape[1]
    o_z = bg_ref.shape[1]
    o_xbc = o_z + d_inner
    n_state = (conv_dim - d_inner) // (2 * SSM_GROUPS)
    heads = d_inner // SSM_HEAD_DIM
    pairs_per_group = heads // SSM_GROUPS // 2
    width = wlx_ref.shape[1]
    k_ssm = cw_ref.shape[0]
    k_lru = lcw_ref.shape[0]
    blocks = [slice(b * PERM_BLOCK, (b + 1) * PERM_BLOCK) for b in range(STEP_BLOCKS)]

    @pl.when(pl.program_id(1) == 0)
    def _():
        ctail_ref[...] = jnp.zeros(ctail_ref.shape, F32)
        st_ref[...] = jnp.zeros(st_ref.shape, F32)
        ltail_ref[...] = jnp.zeros(ltail_ref.shape, F32)
        h_ref[...] = jnp.zeros(h_ref.shape, F32)
        sp_ref[...] = _softplus(-lam_ref[...])

    hn_ref[...] = _permute_rows(slab_ref, _rms(x_ref[...], nw1_ref[...])).astype(BF16)

    def project_lru(lo):
        pcols = slice(lo, lo + 2 * LANES)
        lx_ref[:, pcols] = _dot(hn_ref[...], wlx_ref[:, pcols])
        gy_ref[:, pcols] = _gelu_tanh(_dot(hn_ref[...], wly_ref[:, pcols]))

    def conv_xbc(lo):
        cols = slice(lo, lo + MIX_TILE)
        proj = _dot(hn_ref[...], wm_ref[:, o_xbc + lo:o_xbc + lo + MIX_TILE])
        for rows in blocks:
            cur = proj[rows]
            act_ref[rows, cols] = _silu(_perm_conv(cur, ctail_ref[:, cols], cw_ref[:, cols], cb_ref[:, cols]))
            ctail_ref[:, cols] = _conv_tail(cur, k_ssm)

    def lru_block(j):
        cols = slice(j * LANES, (j + 1) * LANES)
        u_blocks = []
        for rows in blocks:
            cur = lx_ref[rows, cols]
            u_blocks.append(_perm_conv(cur, ltail_ref[:, cols], lcw_ref[:, cols], lcb_ref[:, cols]))
            ltail_ref[:, cols] = _conv_tail(cur, k_lru)
        u = jnp.concatenate(u_blocks, axis=0)
        ri = _dot(u.astype(BF16), wri_ref[j])
        r_gate = _sigmoid(ri[:, :LANES] + br_ref[:, cols])
        i_gate = _sigmoid(ri[:, LANES:] + bi_ref[:, cols])
        log_a = (-LRU_C) * r_gate * sp_ref[:, cols]
        a = jnp.exp(log_a)
        b_in = _sqrt_unit(1.0 - jnp.exp(2.0 * log_a)) * (i_gate * u)
        for rows in blocks:
            h, h_last = _perm_scan(a[rows], b_in[rows], h_ref[:, cols])
            h_ref[:, cols] = h_last
            lr_ref[rows, cols] = (h * gy_ref[rows, cols]).astype(BF16)

    lru_slices = list(range(0, width, 2 * LANES))
    xbc_slices = list(range(0, conv_dim, MIX_TILE))
    n_lru = width // LANES
    project_lru(lru_slices[0])
    for i in range(max(n_lru, len(xbc_slices))):
        if i % 2 == 0 and i // 2 + 1 < len(lru_slices):
            project_lru(lru_slices[i // 2 + 1])
        if i < len(xbc_slices):
            conv_xbc(xbc_slices[i])
        if i < n_lru:
            lru_block(i)

    def side_z(lo):
        cols = slice(lo, lo + SIDE_TILE)
        zs_ref[:, cols] = _silu(_dot(hn_ref[...], wm_ref[:, o_z + lo:o_z + lo + SIDE_TILE]))

    def side_g(lo):
        cols = slice(lo, lo + SIDE_TILE)
        g_ref[:, cols] = _sigmoid(_dot(hn_ref[...], wm_ref[:, cols]) + bg_ref[:, cols])

    side = ([functools.partial(side_z, lo) for lo in range(0, d_inner, SIDE_TILE)]
            + [functools.partial(side_g, lo) for lo in range(0, 2 * d, SIDE_TILE)])
    n_pairs = STEP_BLOCKS * SSM_GROUPS * pairs_per_group
    dt_raw = _dot(hn_ref[...], wdt_ref[...])
    pairs_done = 0
    side_done = 0
    causal = _causal_mask()
    first_half = lax.broadcasted_iota(jnp.int32, (1, LANES), 1) < SSM_HEAD_DIM
    for b, rows in enumerate(blocks):
        _ssd_decays(dt_raw[rows], dtb_ref[...], alog_ref[...], heads, causal, cs_ref.at[b],
                    rows_ref.at[b])
        for g in range(SSM_GROUPS):
            group = _ssd_group(act_ref, rows, g, d_inner, n_state)
            for pair in range(pairs_per_group):
                _ssd_pair(act_ref, rows, g * pairs_per_group + pair, group, causal, first_half,
                          dsk_ref, st_ref, y_ref, cs_ref.at[b], rows_ref.at[b])
                pairs_done += 1
                while side_done * n_pairs < pairs_done * len(side):
                    side[side_done]()
                    side_done += 1

    gw = d_inner // SSM_GROUPS
    for g in range(SSM_GROUPS):
        cols = slice(g * gw, (g + 1) * gw)
        yg = y_ref[:, cols] * zs_ref[:, cols]
        yn_ref[:, cols] = _rms(yg, snw_ref[:, cols]).astype(BF16)

    for lo in range(0, d, MIX_TILE):
        cols = slice(lo, lo + MIX_TILE)
        gcols = slice(d + lo, d + lo + MIX_TILE)
        merged = (g_ref[:, cols] * _dot(yn_ref[...], wos_ref[:, cols])
                  + g_ref[:, gcols] * _dot(lr_ref[...], wol_ref[:, cols]))
        mg_ref[:, cols] = merged.astype(BF16)
    mixed = _dot(mg_ref[...], wo_ref[...])
    out_ref[...] = x_ref[...] + _unpermute_rows(slab_ref, mixed)


def _mixer(x2, batch, seq, weights):
    t, d = x2.shape
    tm = STEP_TOKENS
    nc = seq // tm
    (nw1, wm, bg, wdt, wlx, wly, cw, cb, dtb, alog, dsk, snw,
     lcw, lcb, wri, br, bi, lam, wos, wol, wo) = weights
    d_inner = snw.shape[1]
    conv_dim = cw.shape[1]
    n_state = (conv_dim - d_inner) // (2 * SSM_GROUPS)
    heads = d_inner // SSM_HEAD_DIM
    width = wlx.shape[1]
    row = lambda b, c: (b * nc + c, 0)
    return pl.pallas_call(
        _mixer_kernel,
        grid=(batch, nc),
        in_specs=[pl.BlockSpec((tm, d), row)] + [_resident(w.shape) for w in weights],
        out_specs=pl.BlockSpec((tm, d), row),
        out_shape=jax.ShapeDtypeStruct((t, d), F32),
        scratch_shapes=[
            pltpu.VMEM((d // LANES, tm, LANES), F32),
            pltpu.VMEM((tm, d), BF16),
            pltpu.VMEM((tm, conv_dim + LANES), F32),
            pltpu.VMEM((tm, d_inner + LANES), F32),
            pltpu.VMEM((tm, d_inner), BF16),
            pltpu.VMEM((tm, width), BF16),
            pltpu.VMEM((tm, d), BF16),
            pltpu.VMEM((tm, width + LANES), F32),
            pltpu.VMEM((tm, width + LANES), F32),
            pltpu.VMEM((tm, d_inner + LANES), F32),
            pltpu.VMEM((tm, 2 * d + LANES), F32),
            pltpu.VMEM((cw.shape[0] - 1, conv_dim), F32),
            pltpu.VMEM((d_inner // LANES, n_state, LANES), F32),
            pltpu.VMEM((STEP_BLOCKS, PERM_BLOCK, LANES), F32),
            pltpu.VMEM((STEP_BLOCKS, 2, heads, PERM_BLOCK), F32),
            pltpu.VMEM((lcw.shape[0] - 1, width), F32),
            pltpu.VMEM((1, width), F32),
            pltpu.VMEM((1, width), F32),
        ],
        compiler_params=pltpu.CompilerParams(dimension_semantics=("arbitrary", "arbitrary"),
                                             vmem_limit_bytes=VMEM_LIMIT_BYTES),
        name="mixer",
    )(x2, *weights)


def _ffn_kernel(h_ref, n2_ref, wfi_ref, wfo_ref, nf_ref, out_ref, *, final_norm):
    h = h_ref[...]
    hn = _rms(h, n2_ref[...]).astype(BF16)
    hidden = wfo_ref.shape[0]
    acc = h
    for lo in range(0, hidden, COL_TILE):
        hi = min(lo + COL_TILE, hidden)
        act = (_silu(_dot(hn, wfi_ref[:, lo:hi]))
               * _dot(hn, wfi_ref[:, hidden + lo:hidden + hi])).astype(BF16)
        acc = acc + _dot(act, wfo_ref[lo:hi, :])
    if final_norm:
        acc = _rms(acc, nf_ref[...])
    out_ref[...] = acc


def _ffn(h2, n2, wfi, wfo, nf, final_norm):
    t, d = h2.shape
    tm = FFN_ROWS
    row = lambda i: (i, 0)
    weights = (n2, wfi, wfo, nf)
    return pl.pallas_call(
        functools.partial(_ffn_kernel, final_norm=final_norm),
        grid=(t // tm,),
        in_specs=[pl.BlockSpec((tm, d), row)] + [_resident(w.shape) for w in weights],
        out_specs=pl.BlockSpec((tm, d), row),
        out_shape=jax.ShapeDtypeStruct((t, d), F32),
        compiler_params=pltpu.CompilerParams(dimension_semantics=("arbitrary",),
                                             vmem_limit_bytes=VMEM_LIMIT_BYTES),
        name="ffn",
    )(h2, *weights)


def _split_w_in_kernel(w_ref, wm_ref, wdt_ref, wlx_ref, wly_ref, *, o_dt, o_lx, o_ly):
    wm_ref[...] = w_ref[:, :o_dt].astype(BF16)
    dt = w_ref[:, o_dt:o_lx]
    pad = jnp.zeros((dt.shape[0], LANES - dt.shape[1]), F32)
    wdt_ref[...] = jnp.concatenate([dt, pad], axis=1).astype(BF16)
    wlx_ref[...] = w_ref[:, o_lx:o_ly].astype(BF16)
    wly_ref[...] = w_ref[:, o_ly:].astype(BF16)


def _split_w_in(w_in, layer, o_dt, o_lx, o_ly):
    _, d, n = w_in.shape
    assert o_lx - o_dt <= LANES
    tm = LANES
    widths = (o_dt, LANES, o_ly - o_lx, n - o_ly)
    row = lambda i: (i, 0)
    return pl.pallas_call(
        functools.partial(_split_w_in_kernel, o_dt=o_dt, o_lx=o_lx, o_ly=o_ly),
        grid=(d // tm,),
        in_specs=[pl.BlockSpec((None, tm, n), lambda i: (layer, i, 0))],
        out_specs=[pl.BlockSpec((tm, w), row) for w in widths],
        out_shape=[jax.ShapeDtypeStruct((d, w), BF16) for w in widths],
        compiler_params=pltpu.CompilerParams(dimension_semantics=("arbitrary",),
                                             vmem_limit_bytes=VMEM_LIMIT_BYTES),
        name="split_w_in",
    )(w_in)


def _pad_lanes(a):
    pad = (-a.shape[-1]) % LANES
    return jnp.pad(a, [(0, 0)] * (a.ndim - 1) + [(0, pad)])


def kernel(x, norm1_w, w_in, b_branch_gate, ssm_conv_w, ssm_conv_b, ssm_dt_bias, ssm_a_log, ssm_d, ssm_norm_w, w_out_ssm, lru_conv_w, lru_conv_b, lru_w_r, lru_b_r, lru_w_i, lru_b_i, lru_lambda, w_out_lru, w_out, norm2_w, w_ffn_in, w_ffn_out, norm_f_w):
    batch, seq, d = x.shape
    depth = norm1_w.shape[0]
    d_inner = w_out_ssm.shape[1]
    conv_dim = ssm_conv_w.shape[2]
    heads = ssm_dt_bias.shape[1]
    width = w_out_lru.shape[1]
    hidden = w_ffn_out.shape[1]
    n_gates = b_branch_gate.shape[1]
    assert seq % STEP_TOKENS == 0 and (batch * seq) % FFN_ROWS == 0
    assert d_inner == heads * SSM_HEAD_DIM and n_gates == 2 * d
    assert d % MIX_TILE == 0 and conv_dim % MIX_TILE == 0 and width % (2 * LANES) == 0
    assert (d_inner // SSM_GROUPS) % LANES == 0 and d_inner % SIDE_TILE == 0 and hidden % LANES == 0

    o_z = n_gates
    o_xbc = o_z + d_inner
    o_dt = o_xbc + conv_dim
    o_lx = o_dt + heads
    o_ly = o_lx + width

    h2 = x.reshape(batch * seq, d)
    for l in range(depth):
        wm, wdt, wlx, wly = _split_w_in(w_in, l, o_dt, o_lx, o_ly)
        row2 = lambda a: a[l].reshape(1, -1)
        weights = (
            row2(norm1_w), wm, row2(b_branch_gate), wdt, wlx, wly,
            ssm_conv_w[l], row2(ssm_conv_b), _pad_lanes(row2(ssm_dt_bias)), _pad_lanes(row2(ssm_a_log)),
            jnp.repeat(ssm_d[l], SSM_HEAD_DIM).reshape(1, -1), row2(ssm_norm_w),
            lru_conv_w[l], row2(lru_conv_b),
            jnp.concatenate([lru_w_r[l], lru_w_i[l]], axis=-1).astype(BF16),
            row2(lru_b_r), row2(lru_b_i), row2(lru_lambda),
            w_out_ssm[l].astype(BF16), w_out_lru[l].astype(BF16), w_out[l].astype(BF16))
        h2 = _mixer(h2, batch, seq, weights)
        h2 = _ffn(h2, row2(norm2_w), w_ffn_in[l].astype(BF16), w_ffn_out[l].astype(BF16),
                  norm_f_w.reshape(1, -1), final_norm=(l == depth - 1))
    return h2.reshape(batch, seq, d)
```

```python
import functools

import jax
import jax.numpy as jnp
from jax import lax
from jax.experimental import pallas as pl
from jax.experimental.pallas import tpu as pltpu

F32 = jnp.float32
BF16 = jnp.bfloat16

SSM_GROUPS = 4
SSM_HEAD_DIM = 64
LRU_C = 8.0
RMS_EPS = 1e-6
LOG2_E = 1.4426950408889634

LANES = 128
SUBLANES = 8
VMEM_LIMIT_BYTES = 56 * 1024 * 1024

PERM_BLOCK = 128
GROUP = PERM_BLOCK // SUBLANES
STEP_TOKENS = 256
STEP_BLOCKS = STEP_TOKENS // PERM_BLOCK
FFN_ROWS = 512
COL_TILE = 512
MIX_TILE = 256
SIDE_TILE = 256


def _resident(shape):
    nd = len(shape)
    return pl.BlockSpec(shape, lambda *_: (0,) * nd, pipeline_mode=pl.Buffered(1))


def _softplus(x):
    return jnp.maximum(x, 0.0) + jnp.log1p(jnp.exp(-jnp.abs(x)))


def _sigmoid(x):
    return 0.5 * jnp.tanh(0.5 * x) + 0.5


def _silu(x):
    hx = 0.5 * x
    return hx * jnp.tanh(hx) + hx


def _gelu_tanh(x):
    c0 = 0.7978845608028654
    hx = 0.5 * x
    return hx * jnp.tanh(x * (c0 + (c0 * 0.044715) * (x * x))) + hx


def _sqrt_unit(x):
    return jnp.where(x > 0.0, x * lax.rsqrt(x), 0.0)


def _rms(x, w):
    ms = jnp.mean(x * x, axis=-1, keepdims=True)
    return x * lax.rsqrt(ms + RMS_EPS) * w


def _split3_bf16(v):
    hi = v.astype(BF16)
    r = v - hi.astype(F32)
    mid = r.astype(BF16)
    lo = (r - mid.astype(F32)).astype(BF16)
    return hi, mid, lo


def _dot(a, b):
    return jnp.dot(a, b, preferred_element_type=F32)


def _strided_rows(blk, r):
    return pl.ds(blk * PERM_BLOCK + r, GROUP, stride=SUBLANES)


def _permute_rows(slab_ref, value):
    rows, cols = value.shape
    out = []
    for s in range(cols // LANES):
        slab_ref[s] = value[:, s * LANES:(s + 1) * LANES]
        out.append(jnp.concatenate(
            [slab_ref[s, _strided_rows(blk, r), :]
             for blk in range(rows // PERM_BLOCK) for r in range(SUBLANES)], axis=0))
    return jnp.concatenate(out, axis=1)


def _unpermute_rows(slab_ref, value):
    rows, cols = value.shape
    out = []
    for s in range(cols // LANES):
        for blk in range(rows // PERM_BLOCK):
            for r in range(SUBLANES):
                p = blk * PERM_BLOCK + r * GROUP
                slab_ref[s, _strided_rows(blk, r), :] = value[p:p + GROUP, s * LANES:(s + 1) * LANES]
        out.append(slab_ref[s])
    return jnp.concatenate(out, axis=1)


def _shift_rows(v, s, fill):
    n = v.shape[0]
    if s % SUBLANES == 0:
        head = jnp.broadcast_to(jnp.asarray(fill, v.dtype), (s, v.shape[1]))
        return jnp.concatenate([head, v[:n - s]], axis=0)
    t = lax.broadcasted_iota(jnp.int32, v.shape, 0)
    return jnp.where(t >= s, pltpu.roll(v, s, axis=0), fill)


def _perm_conv(cur, tail, cw, cb):
    k_width = cw.shape[0]
    slabs = [cur[GROUP * r:GROUP * (r + 1)] for r in range(SUBLANES)]
    prev_group = {}
    for d in range(1, k_width):
        q = SUBLANES - d
        prev_group[q] = _shift_rows(slabs[q], 1, tail[k_width - 1 - d:k_width - d, :])
    outs = []
    for r in range(SUBLANES):
        acc = cb
        for k in range(k_width):
            d = k_width - 1 - k
            src = slabs[r - d] if r >= d else prev_group[r - d + SUBLANES]
            acc = acc + cw[k:k + 1, :] * src
        outs.append(acc)
    return jnp.concatenate(outs, axis=0)


def _conv_tail(cur, k_width):
    rows = [GROUP * (SUBLANES - d) + GROUP - 1 for d in range(k_width - 1, 0, -1)]
    return jnp.concatenate([cur[p:p + 1] for p in rows], axis=0)


def _perm_scan(a, b, h_prev):
    pa = [a[0:GROUP]]
    pb = [b[0:GROUP]]
    for r in range(1, SUBLANES):
        ar = a[GROUP * r:GROUP * (r + 1)]
        pa.append(ar * pa[-1])
        pb.append(ar * pb[-1] + b[GROUP * r:GROUP * (r + 1)])
    ga, gb = pa[-1], pb[-1]
    s = 1
    while s < GROUP:
        gb = ga * _shift_rows(gb, s, 0.0) + gb
        ga = ga * _shift_rows(ga, s, 1.0)
        s *= 2
    h_end = ga * h_prev + gb
    h_in = _shift_rows(h_end, 1, h_prev)
    h = jnp.concatenate([pa[r] * h_in + pb[r] for r in range(SUBLANES)], axis=0)
    return h, h_end[GROUP - 1:GROUP]


def _causal_mask():
    lc = PERM_BLOCK
    ri = lax.broadcasted_iota(jnp.int32, (lc, lc), 0)
    ci = lax.broadcasted_iota(jnp.int32, (lc, lc), 1)
    time_of = lambda p: SUBLANES * (p & (GROUP - 1)) + (p >> (GROUP.bit_length() - 1))
    return time_of(ri) >= time_of(ci)


def _ssd_decays(dt_raw, dtb, a_log, heads, causal, cs_ref, rows_ref):
    lc = PERM_BLOCK
    dt = _softplus(dt_raw + dtb)
    d_a = dt * (-jnp.exp(a_log))
    tri = causal.astype(BF16)
    hi, mid, lo = _split3_bf16(d_a)
    cs = (_dot(tri, hi) + _dot(tri, mid) + _dot(tri, lo)) * LOG2_E
    cs_ref[...] = cs
    cs_t = cs.T[:heads]
    dt_t = dt.T[:heads]
    rows_ref[0] = cs_t - jnp.log2(dt_t)
    rows_ref[1] = dt_t * jnp.exp2(cs_t[:, lc - 1:lc] - cs_t)


def _ssd_group(act_ref, rows, g, d_inner, n_state):
    b_off = d_inner + g * n_state
    c_off = d_inner + SSM_GROUPS * n_state + g * n_state
    bm = act_ref[rows, b_off:b_off + n_state]
    cm = act_ref[rows, c_off:c_off + n_state]
    cb = lax.dot_general(cm.astype(BF16), bm.astype(BF16), (((1,), (1,)), ((), ())),
                         preferred_element_type=F32)
    return cm, cb, bm.T


def _ssd_pair(act_ref, rows, j, group, causal, first_half, dsk_ref, st_ref, y_ref, cs_ref, rows_ref):
    lc = PERM_BLOCK
    cm, cb, bm_t = group
    cols = slice(j * LANES, (j + 1) * LANES)
    xs = act_ref[rows, cols]
    s_old = st_ref[j]
    xs_b = xs.astype(BF16)
    rhs = jnp.concatenate([xs_b, s_old.astype(BF16)], axis=0)
    ys, upds, tots = [], [], []
    for k in range(2):
        h = 2 * j + k
        col = jnp.broadcast_to(cs_ref[:, h:h + 1], (lc, lc))
        row = rows_ref[0, h:h + 1, :]
        seg = jnp.exp2(jnp.where(causal, col - row, -jnp.inf))
        g_mat = (cb * seg).astype(BF16)
        c_mat = (cm * jnp.exp2(col)).astype(BF16)
        ys.append(_dot(jnp.concatenate([g_mat, c_mat], axis=1), rhs))
        bw = (bm_t * rows_ref[1, h:h + 1, :]).astype(BF16)
        upds.append(_dot(bw, xs_b))
        tots.append(jnp.exp2(col[lc - 1:lc, :]))
    y_ref[rows, cols] = xs * dsk_ref[:, cols] + jnp.where(first_half, ys[0], ys[1])
    st_ref[j] = (s_old * jnp.where(first_half, tots[0], tots[1])
                 + jnp.where(first_half, upds[0], upds[1]))


def _mixer_kernel(x_ref, nw1_ref, wm_ref, bg_ref, wdt_ref, wlx_ref, wly_ref,
                  cw_ref, cb_ref, dtb_ref, alog_ref, dsk_ref, snw_ref,
                  lcw_ref, lcb_ref, wri_ref, br_ref, bi_ref, lam_ref,
                  wos_ref, wol_ref, wo_ref,
                  out_ref,
                  slab_ref, hn_ref, act_ref, y_ref, yn_ref, lr_ref, mg_ref, lx_ref, gy_ref,
                  zs_ref, g_ref, ctail_ref, st_ref, cs_ref, rows_ref, ltail_ref, h_ref, sp_ref):
    d = x_ref.shape[1]
    d_inner = snw_ref.shape[1]
    conv_dim = cw_ref.shape[1]
    o_z = bg_ref.shape[1]
    o_xbc = o_z + d_inner
    n_state = (conv_dim - d_inner) // (2 * SSM_GROUPS)
    heads = d_inner // SSM_HEAD_DIM
    pairs_per_group = heads // SSM_GROUPS // 2
    width = wlx_ref.shape[1]
    k_ssm = cw_ref.shape[0]
    k_lru = lcw_ref.shape[0]
    blocks = [slice(b * PERM_BLOCK, (b + 1) * PERM_BLOCK) for b in range(STEP_BLOCKS)]

    @pl.when(pl.program_id(1) == 0)
    def _():
        ctail_ref[...] = jnp.zeros(ctail_ref.shape, F32)
        st_ref[...] = jnp.zeros(st_ref.shape, F32)
        ltail_ref[...] = jnp.zeros(ltail_ref.shape, F32)
        h_ref[...] = jnp.zeros(h_ref.shape, F32)
        sp_ref[...] = _softplus(-lam_ref[...])

    hn_ref[...] = _permute_rows(slab_ref, _rms(x_ref[...], nw1_ref[...])).astype(BF16)

    def project_lru(lo):
        pcols = slice(lo, lo + 2 * LANES)
        lx_ref[:, pcols] = _dot(hn_ref[...], wlx_ref[:, pcols])
        gy_ref[:, pcols] = _gelu_tanh(_dot(hn_ref[...], wly_ref[:, pcols]))

    def conv_xbc(lo):
        cols = slice(lo, lo + MIX_TILE)
        proj = _dot(hn_ref[...], wm_ref[:, o_xbc + lo:o_xbc + lo + MIX_TILE])
        for rows in blocks:
            cur = proj[rows]
            act_ref[rows, cols] = _silu(_perm_conv(cur, ctail_ref[:, cols], cw_ref[:, cols], cb_ref[:, cols]))
            ctail_ref[:, cols] = _conv_tail(cur, k_ssm)

    def lru_block(j):
        cols = slice(j * LANES, (j + 1) * LANES)
        u_blocks = []
        for rows in blocks:
            cur = lx_ref[rows, cols]
            u_blocks.append(_perm_conv(cur, ltail_ref[:, cols], lcw_ref[:, cols], lcb_ref[:, cols]))
            ltail_ref[:, cols] = _conv_tail(cur, k_lru)
        u = jnp.concatenate(u_blocks, axis=0)
        ri = _dot(u.astype(BF16), wri_ref[j])
        r_gate = _sigmoid(ri[:, :LANES] + br_ref[:, cols])
        i_gate = _sigmoid(ri[:, LANES:] + bi_ref[:, cols])
        log_a = (-LRU_C) * r_gate * sp_ref[:, cols]
        a = jnp.exp(log_a)
        b_in = _sqrt_unit(1.0 - jnp.exp(2.0 * log_a)) * (i_gate * u)
        for rows in blocks:
            h, h_last = _perm_scan(a[rows], b_in[rows], h_ref[:, cols])
            h_ref[:, cols] = h_last
            lr_ref[rows, cols] = (h * gy_ref[rows, cols]).astype(BF16)

    lru_slices = list(range(0, width, 2 * LANES))
    xbc_slices = list(range(0, conv_dim, MIX_TILE))
    n_lru = width // LANES
    project_lru(lru_slices[0])
    for i in range(max(n_lru, len(xbc_slices))):
        if i % 2 == 0 and i // 2 + 1 < len(lru_slices):
            project_lru(lru_slices[i // 2 + 1])
        if i < len(xbc_slices):
            conv_xbc(xbc_slices[i])
        if i < n_lru:
            lru_block(i)

    def side_z(lo):
        cols = slice(lo, lo + SIDE_TILE)
        zs_ref[:, cols] = _silu(_dot(hn_ref[...], wm_ref[:, o_z + lo:o_z + lo + SIDE_TILE]))

    def side_g(lo):
        cols = slice(lo, lo + SIDE_TILE)
        g_ref[:, cols] = _sigmoid(_dot(hn_ref[...], wm_ref[:, cols]) + bg_ref[:, cols])

    side = ([functools.partial(side_z, lo) for lo in range(0, d_inner, SIDE_TILE)]
            + [functools.partial(side_g, lo) for lo in range(0, 2 * d, SIDE_TILE)])
    n_pairs = STEP_BLOCKS * SSM_GROUPS * pairs_per_group
    dt_raw = _dot(hn_ref[...], wdt_ref[...])
    pairs_done = 0
    side_done = 0
    causal = _causal_mask()
    first_half = lax.broadcasted_iota(jnp.int32, (1, LANES), 1) < SSM_HEAD_DIM
    for b, rows in enumerate(blocks):
        _ssd_decays(dt_raw[rows], dtb_ref[...], alog_ref[...], heads, causal, cs_ref.at[b],
                    rows_ref.at[b])
        for g in range(SSM_GROUPS):
            group = _ssd_group(act_ref, rows, g, d_inner, n_state)
            for pair in range(pairs_per_group):
                _ssd_pair(act_ref, rows, g * pairs_per_group + pair, group, causal, first_half,
                          dsk_ref, st_ref, y_ref, cs_ref.at[b], rows_ref.at[b])
                pairs_done += 1
                while side_done * n_pairs < pairs_done * len(side):
                    side[side_done]()
                    side_done += 1

    gw = d_inner // SSM_GROUPS
    for g in range(SSM_GROUPS):
        cols = slice(g * gw, (g + 1) * gw)
        yg = y_ref[:, cols] * zs_ref[:, cols]
        yn_ref[:, cols] = _rms(yg, snw_ref[:, cols]).astype(BF16)

    for lo in range(0, d, MIX_TILE):
        cols = slice(lo, lo + MIX_TILE)
        gcols = slice(d + lo, d + lo + MIX_TILE)
        merged = (g_ref[:, cols] * _dot(yn_ref[...], wos_ref[:, cols])
                  + g_ref[:, gcols] * _dot(lr_ref[...], wol_ref[:, cols]))
        mg_ref[:, cols] = merged.astype(BF16)
    mixed = _dot(mg_ref[...], wo_ref[...])
    out_ref[...] = x_ref[...] + _unpermute_rows(slab_ref, mixed)


def _mixer(x2, batch, seq, weights):
    t, d = x2.shape
    tm = STEP_TOKENS
    nc = seq // tm
    (nw1, wm, bg, wdt, wlx, wly, cw, cb, dtb, alog, dsk, snw,
     lcw, lcb, wri, br, bi, lam, wos, wol, wo) = weights
    d_inner = snw.shape[1]
    conv_dim = cw.shape[1]
    n_state = (conv_dim - d_inner) // (2 * SSM_GROUPS)
    heads = d_inner // SSM_HEAD_DIM
    width = wlx.shape[1]
    row = lambda b, c: (b * nc + c, 0)
    return pl.pallas_call(
        _mixer_kernel,
        grid=(batch, nc),
        in_specs=[pl.BlockSpec((tm, d), row)] + [_resident(w.shape) for w in weights],
        out_specs=pl.BlockSpec((tm, d), row),
        out_shape=jax.ShapeDtypeStruct((t, d), F32),
        scratch_shapes=[
            pltpu.VMEM((d // LANES, tm, LANES), F32),
            pltpu.VMEM((tm, d), BF16),
            pltpu.VMEM((tm, conv_dim + LANES), F32),
            pltpu.VMEM((tm, d_inner + LANES), F32),
            pltpu.VMEM((tm, d_inner), BF16),
            pltpu.VMEM((tm, width), BF16),
            pltpu.VMEM((tm, d), BF16),
            pltpu.VMEM((tm, width + LANES), F32),
            pltpu.VMEM((tm, width + LANES), F32),
            pltpu.VMEM((tm, d_inner + LANES), F32),
            pltpu.VMEM((tm, 2 * d + LANES), F32),
            pltpu.VMEM((cw.shape[0] - 1, conv_dim), F32),
            pltpu.VMEM((d_inner // LANES, n_state, LANES), F32),
            pltpu.VMEM((STEP_BLOCKS, PERM_BLOCK, LANES), F32),
            pltpu.VMEM((STEP_BLOCKS, 2, heads, PERM_BLOCK), F32),
            pltpu.VMEM((lcw.shape[0] - 1, width), F32),
            pltpu.VMEM((1, width), F32),
            pltpu.VMEM((1, width), F32),
        ],
        compiler_params=pltpu.CompilerParams(dimension_semantics=("arbitrary", "arbitrary"),
                                             vmem_limit_bytes=VMEM_LIMIT_BYTES),
        name="mixer",
    )(x2, *weights)


def _ffn_kernel(h_ref, n2_ref, wfi_ref, wfo_ref, nf_ref, out_ref, *, final_norm):
    h = h_ref[...]
    hn = _rms(h, n2_ref[...]).astype(BF16)
    hidden = wfo_ref.shape[0]
    acc = h
    for lo in range(0, hidden, COL_TILE):
        hi = min(lo + COL_TILE, hidden)
        act = (_silu(_dot(hn, wfi_ref[:, lo:hi]))
               * _dot(hn, wfi_ref[:, hidden + lo:hidden + hi])).astype(BF16)
        acc = acc + _dot(act, wfo_ref[lo:hi, :])
    if final_norm:
        acc = _rms(acc, nf_ref[...])
    out_ref[...] = acc


def _ffn(h2, n2, wfi, wfo, nf, final_norm):
    t, d = h2.shape
    tm = FFN_ROWS
    row = lambda i: (i, 0)
    weights = (n2, wfi, wfo, nf)
    return pl.pallas_call(
        functools.partial(_ffn_kernel, final_norm=final_norm),
        grid=(t // tm,),
        in_specs=[pl.BlockSpec((tm, d), row)] + [_resident(w.shape) for w in weights],
        out_specs=pl.BlockSpec((tm, d), row),
        out_shape=jax.ShapeDtypeStruct((t, d), F32),
        compiler_params=pltpu.CompilerParams(dimension_semantics=("arbitrary",),
                                             vmem_limit_bytes=VMEM_LIMIT_BYTES),
        name="ffn",
    )(h2, *weights)


def _cast_main_kernel(wt_ref, out_ref):
    out_ref[...] = wt_ref[...].T.astype(BF16)


def _cast_tail_kernel(wt_ref, wdt_ref, wlx_ref, wly_ref, *, n_dt, n_lx):
    dt = wt_ref[0:n_dt, :]
    dt = jnp.concatenate([dt, jnp.zeros((LANES - n_dt, dt.shape[1]), F32)], axis=0)
    wdt_ref[...] = dt.T.astype(BF16)
    wlx_ref[...] = wt_ref[n_dt:n_dt + n_lx, :].T.astype(BF16)
    wly_ref[...] = wt_ref[n_dt + n_lx:, :].T.astype(BF16)


def _split_w_in(wt, o_dt, o_lx, o_ly):
    n, d = wt.shape
    tn = 2 * LANES
    assert o_dt % tn == 0 and o_lx - o_dt <= LANES and o_lx % SUBLANES == 0 and o_ly % SUBLANES == 0
    params = pltpu.CompilerParams(dimension_semantics=("arbitrary",), vmem_limit_bytes=VMEM_LIMIT_BYTES)
    wm = pl.pallas_call(
        _cast_main_kernel,
        grid=(o_dt // tn,),
        in_specs=[pl.BlockSpec((tn, d), lambda i: (i, 0))],
        out_specs=pl.BlockSpec((d, tn), lambda i: (0, i)),
        out_shape=jax.ShapeDtypeStruct((d, o_dt), BF16),
        compiler_params=params,
        name="cast_w_in_main",
    )(wt)
    widths = (LANES, o_ly - o_lx, n - o_ly)
    wdt, wlx, wly = pl.pallas_call(
        functools.partial(_cast_tail_kernel, n_dt=o_lx - o_dt, n_lx=o_ly - o_lx),
        grid=(1,),
        in_specs=[pl.BlockSpec((pl.Element(n - o_dt), pl.Element(d)), lambda i: (o_dt, 0))],
        out_specs=[pl.BlockSpec((d, w), lambda i: (0, 0)) for w in widths],
        out_shape=[jax.ShapeDtypeStruct((d, w), BF16) for w in widths],
        compiler_params=params,
        name="cast_w_in_tail",
    )(wt)
    return wm, wdt, wlx, wly


def _pad_lanes(a):
    pad = (-a.shape[-1]) % LANES
    return jnp.pad(a, [(0, 0)] * (a.ndim - 1) + [(0, pad)])


def kernel(x, norm1_w, w_in, b_branch_gate, ssm_conv_w, ssm_conv_b, ssm_dt_bias, ssm_a_log, ssm_d, ssm_norm_w, w_out_ssm, lru_conv_w, lru_conv_b, lru_w_r, lru_b_r, lru_w_i, lru_b_i, lru_lambda, w_out_lru, w_out, norm2_w, w_ffn_in, w_ffn_out, norm_f_w):
    batch, seq, d = x.shape
    depth = norm1_w.shape[0]
    d_inner = w_out_ssm.shape[1]
    conv_dim = ssm_conv_w.shape[2]
    heads = ssm_dt_bias.shape[1]
    width = w_out_lru.shape[1]
    hidden = w_ffn_out.shape[1]
    n_gates = b_branch_gate.shape[1]
    assert seq % STEP_TOKENS == 0 and (batch * seq) % FFN_ROWS == 0
    assert d_inner == heads * SSM_HEAD_DIM and n_gates == 2 * d
    assert d % MIX_TILE == 0 and conv_dim % MIX_TILE == 0 and width % (2 * LANES) == 0
    assert (d_inner // SSM_GROUPS) % LANES == 0 and d_inner % SIDE_TILE == 0 and hidden % LANES == 0

    o_z = n_gates
    o_xbc = o_z + d_inner
    o_dt = o_xbc + conv_dim
    o_lx = o_dt + heads
    o_ly = o_lx + width

    h2 = x.reshape(batch * seq, d)
    for l in range(depth):
        wm, wdt, wlx, wly = _split_w_in(jnp.swapaxes(w_in, 1, 2)[l], o_dt, o_lx, o_ly)
        row2 = lambda a: a[l].reshape(1, -1)
        weights = (
            row2(norm1_w), wm, row2(b_branch_gate), wdt, wlx, wly,
            ssm_conv_w[l], row2(ssm_conv_b), _pad_lanes(row2(ssm_dt_bias)), _pad_lanes(row2(ssm_a_log)),
            jnp.repeat(ssm_d[l], SSM_HEAD_DIM).reshape(1, -1), row2(ssm_norm_w),
            lru_conv_w[l], row2(lru_conv_b),
            jnp.concatenate([lru_w_r[l], lru_w_i[l]], axis=-1).astype(BF16),
            row2(lru_b_r), row2(lru_b_i), row2(lru_lambda),
            w_out_ssm[l].astype(BF16), w_out_lru[l].astype(BF16), w_out[l].astype(BF16))
        h2 = _mixer(h2, batch, seq, weights)
        h2 = _ffn(h2, row2(norm2_w), w_ffn_in[l].astype(BF16), w_ffn_out[l].astype(BF16),
                  norm_f_w.reshape(1, -1), final_norm=(l == depth - 1))
    return h2.reshape(batch, seq, d)
```

```python
import functools

import jax
import jax.numpy as jnp
from jax import lax
from jax.experimental import pallas as pl
from jax.experimental.pallas import tpu as pltpu

F32 = jnp.float32
BF16 = jnp.bfloat16

SSM_GROUPS = 4
SSM_HEAD_DIM = 64
LRU_C = 8.0
RMS_EPS = 1e-6
LOG2_E = 1.4426950408889634

LANES = 128
SUBLANES = 8
VMEM_LIMIT_BYTES = 56 * 1024 * 1024

PERM_BLOCK = 128
GROUP = PERM_BLOCK // SUBLANES
STEP_TOKENS = 512
STEP_BLOCKS = STEP_TOKENS // PERM_BLOCK
FFN_ROWS = 512
COL_TILE = 512
MIX_TILE = 256
SIDE_TILE = 256


def _resident(shape):
    nd = len(shape)
    return pl.BlockSpec(shape, lambda *_: (0,) * nd, pipeline_mode=pl.Buffered(1))


def _softplus(x):
    return jnp.maximum(x, 0.0) + jnp.log1p(jnp.exp(-jnp.abs(x)))


def _sigmoid(x):
    return 0.5 * jnp.tanh(0.5 * x) + 0.5


def _silu(x):
    hx = 0.5 * x
    return hx * jnp.tanh(hx) + hx


def _gelu_tanh(x):
    c0 = 0.7978845608028654
    hx = 0.5 * x
    return hx * jnp.tanh(x * (c0 + (c0 * 0.044715) * (x * x))) + hx


def _sqrt_unit(x):
    return jnp.where(x > 0.0, x * lax.rsqrt(x), 0.0)


def _rms(x, w):
    ms = jnp.mean(x * x, axis=-1, keepdims=True)
    return x * lax.rsqrt(ms + RMS_EPS) * w


def _split3_bf16(v):
    hi = v.astype(BF16)
    r = v - hi.astype(F32)
    mid = r.astype(BF16)
    lo = (r - mid.astype(F32)).astype(BF16)
    return hi, mid, lo


def _dot(a, b):
    return jnp.dot(a, b, preferred_element_type=F32)


def _strided_rows(blk, r):
    return pl.ds(blk * PERM_BLOCK + r, GROUP, stride=SUBLANES)


def _permute_rows(slab_ref, value):
    rows, cols = value.shape
    out = []
    for s in range(cols // LANES):
        slab_ref[s] = value[:, s * LANES:(s + 1) * LANES]
        out.append(jnp.concatenate(
            [slab_ref[s, _strided_rows(blk, r), :]
             for blk in range(rows // PERM_BLOCK) for r in range(SUBLANES)], axis=0))
    return jnp.concatenate(out, axis=1)


def _unpermute_rows(slab_ref, value):
    rows, cols = value.shape
    out = []
    for s in range(cols // LANES):
        for blk in range(rows // PERM_BLOCK):
            for r in range(SUBLANES):
                p = blk * PERM_BLOCK + r * GROUP
                slab_ref[s, _strided_rows(blk, r), :] = value[p:p + GROUP, s * LANES:(s + 1) * LANES]
        out.append(slab_ref[s])
    return jnp.concatenate(out, axis=1)


def _shift_rows(v, s, fill):
    n = v.shape[0]
    if s % SUBLANES == 0:
        head = jnp.broadcast_to(jnp.asarray(fill, v.dtype), (s, v.shape[1]))
        return jnp.concatenate([head, v[:n - s]], axis=0)
    t = lax.broadcasted_iota(jnp.int32, v.shape, 0)
    return jnp.where(t >= s, pltpu.roll(v, s, axis=0), fill)


def _perm_conv(cur, tail, cw, cb):
    k_width = cw.shape[0]
    slabs = [cur[GROUP * r:GROUP * (r + 1)] for r in range(SUBLANES)]
    prev_group = {}
    for d in range(1, k_width):
        q = SUBLANES - d
        prev_group[q] = _shift_rows(slabs[q], 1, tail[k_width - 1 - d:k_width - d, :])
    outs = []
    for r in range(SUBLANES):
        acc = cb
        for k in range(k_width):
            d = k_width - 1 - k
            src = slabs[r - d] if r >= d else prev_group[r - d + SUBLANES]
            acc = acc + cw[k:k + 1, :] * src
        outs.append(acc)
    return jnp.concatenate(outs, axis=0)


def _conv_tail(cur, k_width):
    rows = [GROUP * (SUBLANES - d) + GROUP - 1 for d in range(k_width - 1, 0, -1)]
    return jnp.concatenate([cur[p:p + 1] for p in rows], axis=0)


def _perm_scan(a, b, h_prev):
    pa = [a[0:GROUP]]
    pb = [b[0:GROUP]]
    for r in range(1, SUBLANES):
        ar = a[GROUP * r:GROUP * (r + 1)]
        pa.append(ar * pa[-1])
        pb.append(ar * pb[-1] + b[GROUP * r:GROUP * (r + 1)])
    ga, gb = pa[-1], pb[-1]
    s = 1
    while s < GROUP:
        gb = ga * _shift_rows(gb, s, 0.0) + gb
        ga = ga * _shift_rows(ga, s, 1.0)
        s *= 2
    h_end = ga * h_prev + gb
    h_in = _shift_rows(h_end, 1, h_prev)
    h = jnp.concatenate([pa[r] * h_in + pb[r] for r in range(SUBLANES)], axis=0)
    return h, h_end[GROUP - 1:GROUP]


def _causal_mask():
    lc = PERM_BLOCK
    ri = lax.broadcasted_iota(jnp.int32, (lc, lc), 0)
    ci = lax.broadcasted_iota(jnp.int32, (lc, lc), 1)
    time_of = lambda p: SUBLANES * (p & (GROUP - 1)) + (p >> (GROUP.bit_length() - 1))
    return time_of(ri) >= time_of(ci)


def _ssd_decays(dt_raw, dtb, a_log, heads, causal, cs_ref, rows_ref):
    lc = PERM_BLOCK
    dt = _softplus(dt_raw + dtb)
    d_a = dt * (-jnp.exp(a_log))
    tri = causal.astype(BF16)
    hi, mid, lo = _split3_bf16(d_a)
    cs = (_dot(tri, hi) + _dot(tri, mid) + _dot(tri, lo)) * LOG2_E
    cs_ref[...] = cs
    cs_t = cs.T[:heads]
    dt_t = dt.T[:heads]
    rows_ref[0] = cs_t - jnp.log2(dt_t)
    rows_ref[1] = dt_t * jnp.exp2(cs_t[:, lc - 1:lc] - cs_t)


def _ssd_group(act_ref, rows, g, d_inner, n_state):
    b_off = d_inner + g * n_state
    c_off = d_inner + SSM_GROUPS * n_state + g * n_state
    bm = act_ref[rows, b_off:b_off + n_state]
    cm = act_ref[rows, c_off:c_off + n_state]
    cb = lax.dot_general(cm.astype(BF16), bm.astype(BF16), (((1,), (1,)), ((), ())),
                         preferred_element_type=F32)
    return cm, cb, bm.T


def _ssd_pair(act_ref, rows, j, group, causal, first_half, dsk_ref, st_ref, y_ref, cs_ref, rows_ref):
    lc = PERM_BLOCK
    cm, cb, bm_t = group
    cols = slice(j * LANES, (j + 1) * LANES)
    xs = act_ref[rows, cols]
    s_old = st_ref[j]
    xs_b = xs.astype(BF16)
    rhs = jnp.concatenate([xs_b, s_old.astype(BF16)], axis=0)
    ys, upds, tots = [], [], []
    for k in range(2):
        h = 2 * j + k
        col = jnp.broadcast_to(cs_ref[:, h:h + 1], (lc, lc))
        row = rows_ref[0, h:h + 1, :]
        seg = jnp.exp2(jnp.where(causal, col - row, -jnp.inf))
        g_mat = (cb * seg).astype(BF16)
        c_mat = (cm * jnp.exp2(col)).astype(BF16)
        ys.append(_dot(jnp.concatenate([g_mat, c_mat], axis=1), rhs))
        bw = (bm_t * rows_ref[1, h:h + 1, :]).astype(BF16)
        upds.append(_dot(bw, xs_b))
        tots.append(jnp.exp2(col[lc - 1:lc, :]))
    y_ref[rows, cols] = xs * dsk_ref[:, cols] + jnp.where(first_half, ys[0], ys[1])
    st_ref[j] = (s_old * jnp.where(first_half, tots[0], tots[1])
                 + jnp.where(first_half, upds[0], upds[1]))


def _ssd_branch_kernel(x_ref, nw1_ref, wzx_ref, wdt_ref, cw_ref, cb_ref, dtb_ref, alog_ref, dsk_ref,
                       snw_ref, wos_ref,
                       out_ref,
                       slab_ref, hn_ref, act_ref, y_ref, yn_ref, zs_ref, ctail_ref, st_ref, cs_ref,
                       rows_ref):
    d = x_ref.shape[1]
    d_inner = snw_ref.shape[1]
    conv_dim = cw_ref.shape[1]
    o_xbc = d_inner
    n_state = (conv_dim - d_inner) // (2 * SSM_GROUPS)
    heads = d_inner // SSM_HEAD_DIM
    pairs_per_group = heads // SSM_GROUPS // 2
    k_ssm = cw_ref.shape[0]
    blocks = [slice(b * PERM_BLOCK, (b + 1) * PERM_BLOCK) for b in range(STEP_BLOCKS)]

    @pl.when(pl.program_id(1) == 0)
    def _():
        ctail_ref[...] = jnp.zeros(ctail_ref.shape, F32)
        st_ref[...] = jnp.zeros(st_ref.shape, F32)

    hn_ref[...] = _permute_rows(slab_ref, _rms(x_ref[...], nw1_ref[...])).astype(BF16)

    for lo in range(0, conv_dim, MIX_TILE):
        cols = slice(lo, lo + MIX_TILE)
        proj = _dot(hn_ref[...], wzx_ref[:, o_xbc + lo:o_xbc + lo + MIX_TILE])
        for rows in blocks:
            cur = proj[rows]
            act_ref[rows, cols] = _silu(_perm_conv(cur, ctail_ref[:, cols], cw_ref[:, cols], cb_ref[:, cols]))
            ctail_ref[:, cols] = _conv_tail(cur, k_ssm)

    def side_z(lo):
        cols = slice(lo, lo + SIDE_TILE)
        zs_ref[:, cols] = _silu(_dot(hn_ref[...], wzx_ref[:, cols]))

    side = [functools.partial(side_z, lo) for lo in range(0, d_inner, SIDE_TILE)]
    n_pairs = STEP_BLOCKS * SSM_GROUPS * pairs_per_group
    dt_raw = _dot(hn_ref[...], wdt_ref[...])
    pairs_done = 0
    side_done = 0
    causal = _causal_mask()
    first_half = lax.broadcasted_iota(jnp.int32, (1, LANES), 1) < SSM_HEAD_DIM
    for b, rows in enumerate(blocks):
        _ssd_decays(dt_raw[rows], dtb_ref[...], alog_ref[...], heads, causal, cs_ref.at[b],
                    rows_ref.at[b])
        for g in range(SSM_GROUPS):
            group = _ssd_group(act_ref, rows, g, d_inner, n_state)
            for pair in range(pairs_per_group):
                _ssd_pair(act_ref, rows, g * pairs_per_group + pair, group, causal, first_half,
                          dsk_ref, st_ref, y_ref, cs_ref.at[b], rows_ref.at[b])
                pairs_done += 1
                while side_done * n_pairs < pairs_done * len(side):
                    side[side_done]()
                    side_done += 1

    gw = d_inner // SSM_GROUPS
    for g in range(SSM_GROUPS):
        cols = slice(g * gw, (g + 1) * gw)
        yg = y_ref[:, cols] * zs_ref[:, cols]
        yn_ref[:, cols] = _rms(yg, snw_ref[:, cols]).astype(BF16)
    for lo in range(0, d, MIX_TILE):
        cols = slice(lo, lo + MIX_TILE)
        out_ref[:, cols] = _dot(yn_ref[...], wos_ref[:, cols]).astype(out_ref.dtype)


def _lru_branch_kernel(x_ref, ys_ref, nw1_ref, wg_ref, bg_ref, wlx_ref, wly_ref,
                       lcw_ref, lcb_ref, wri_ref, br_ref, bi_ref, lam_ref, wol_ref, wo_ref,
                       out_ref,
                       slab_ref, hn_ref, lr_ref, mg_ref, lx_ref, gy_ref, g_ref, ltail_ref, h_ref,
                       sp_ref):
    d = x_ref.shape[1]
    width = wlx_ref.shape[1]
    k_lru = lcw_ref.shape[0]
    blocks = [slice(b * PERM_BLOCK, (b + 1) * PERM_BLOCK) for b in range(STEP_BLOCKS)]

    @pl.when(pl.program_id(1) == 0)
    def _():
        ltail_ref[...] = jnp.zeros(ltail_ref.shape, F32)
        h_ref[...] = jnp.zeros(h_ref.shape, F32)
        sp_ref[...] = _softplus(-lam_ref[...])

    hn_ref[...] = _permute_rows(slab_ref, _rms(x_ref[...], nw1_ref[...])).astype(BF16)

    def project_lru(lo):
        pcols = slice(lo, lo + 2 * LANES)
        lx_ref[:, pcols] = _dot(hn_ref[...], wlx_ref[:, pcols])
        gy_ref[:, pcols] = _gelu_tanh(_dot(hn_ref[...], wly_ref[:, pcols]))

    def side_g(lo):
        cols = slice(lo, lo + SIDE_TILE)
        g_ref[:, cols] = _sigmoid(_dot(hn_ref[...], wg_ref[:, cols]) + bg_ref[:, cols])

    def lru_block(j):
        cols = slice(j * LANES, (j + 1) * LANES)
        u_blocks = []
        for rows in blocks:
            cur = lx_ref[rows, cols]
            u_blocks.append(_perm_conv(cur, ltail_ref[:, cols], lcw_ref[:, cols], lcb_ref[:, cols]))
            ltail_ref[:, cols] = _conv_tail(cur, k_lru)
        u = jnp.concatenate(u_blocks, axis=0)
        ri = _dot(u.astype(BF16), wri_ref[j])
        r_gate = _sigmoid(ri[:, :LANES] + br_ref[:, cols])
        i_gate = _sigmoid(ri[:, LANES:] + bi_ref[:, cols])
        log_a = (-LRU_C) * r_gate * sp_ref[:, cols]
        a = jnp.exp(log_a)
        b_in = _sqrt_unit(1.0 - jnp.exp(2.0 * log_a)) * (i_gate * u)
        for rows in blocks:
            h, h_last = _perm_scan(a[rows], b_in[rows], h_ref[:, cols])
            h_ref[:, cols] = h_last
            lr_ref[rows, cols] = (h * gy_ref[rows, cols]).astype(BF16)

    lru_slices = list(range(0, width, 2 * LANES))
    g_tiles = list(range(0, 2 * d, SIDE_TILE))
    n_lru = width // LANES
    g_done = 0
    project_lru(lru_slices[0])
    for i in range(n_lru):
        if i % 2 == 0 and i // 2 + 1 < len(lru_slices):
            project_lru(lru_slices[i // 2 + 1])
        lru_block(i)
        while g_done * n_lru < (i + 1) * len(g_tiles):
            side_g(g_tiles[g_done])
            g_done += 1

    for lo in range(0, d, MIX_TILE):
        cols = slice(lo, lo + MIX_TILE)
        gcols = slice(d + lo, d + lo + MIX_TILE)
        merged = (g_ref[:, cols] * ys_ref[:, cols].astype(F32)
                  + g_ref[:, gcols] * _dot(lr_ref[...], wol_ref[:, cols]))
        mg_ref[:, cols] = merged.astype(BF16)
    mixed = _dot(mg_ref[...], wo_ref[...])
    out_ref[...] = x_ref[...] + _unpermute_rows(slab_ref, mixed)


def _ssd_branch(x2, batch, seq, weights):
    t, d = x2.shape
    tm = STEP_TOKENS
    nc = seq // tm
    nw1, wzx, wdt, cw, cb, dtb, alog, dsk, snw, wos = weights
    d_inner = snw.shape[1]
    conv_dim = cw.shape[1]
    n_state = (conv_dim - d_inner) // (2 * SSM_GROUPS)
    heads = d_inner // SSM_HEAD_DIM
    row = lambda b, c: (b * nc + c, 0)
    return pl.pallas_call(
        _ssd_branch_kernel,
        grid=(batch, nc),
        in_specs=[pl.BlockSpec((tm, d), row)] + [_resident(w.shape) for w in weights],
        out_specs=pl.BlockSpec((tm, d), row),
        out_shape=jax.ShapeDtypeStruct((t, d), BF16),
        scratch_shapes=[
            pltpu.VMEM((d // LANES, tm, LANES), F32),
            pltpu.VMEM((tm, d), BF16),
            pltpu.VMEM((tm, conv_dim + LANES), F32),
            pltpu.VMEM((tm, d_inner + LANES), F32),
            pltpu.VMEM((tm, d_inner), BF16),
            pltpu.VMEM((tm, d_inner + LANES), F32),
            pltpu.VMEM((cw.shape[0] - 1, conv_dim), F32),
            pltpu.VMEM((d_inner // LANES, n_state, LANES), F32),
            pltpu.VMEM((STEP_BLOCKS, PERM_BLOCK, LANES), F32),
            pltpu.VMEM((STEP_BLOCKS, 2, heads, PERM_BLOCK), F32),
        ],
        compiler_params=pltpu.CompilerParams(dimension_semantics=("arbitrary", "arbitrary"),
                                             vmem_limit_bytes=VMEM_LIMIT_BYTES),
        name="ssd_branch",
    )(x2, *weights)


def _lru_branch(x2, ys, batch, seq, weights):
    t, d = x2.shape
    tm = STEP_TOKENS
    nc = seq // tm
    nw1, wg, bg, wlx, wly, lcw, lcb, wri, br, bi, lam, wol, wo = weights
    width = wlx.shape[1]
    row = lambda b, c: (b * nc + c, 0)
    return pl.pallas_call(
        _lru_branch_kernel,
        grid=(batch, nc),
        in_specs=([pl.BlockSpec((tm, d), row), pl.BlockSpec((tm, d), row)]
                  + [_resident(w.shape) for w in weights]),
        out_specs=pl.BlockSpec((tm, d), row),
        out_shape=jax.ShapeDtypeStruct((t, d), F32),
        scratch_shapes=[
            pltpu.VMEM((d // LANES, tm, LANES), F32),
            pltpu.VMEM((tm, d), BF16),
            pltpu.VMEM((tm, width), BF16),
            pltpu.VMEM((tm, d), BF16),
            pltpu.VMEM((tm, width + LANES), F32),
            pltpu.VMEM((tm, width + LANES), F32),
            pltpu.VMEM((tm, 2 * d + LANES), F32),
            pltpu.VMEM((lcw.shape[0] - 1, width), F32),
            pltpu.VMEM((1, width), F32),
            pltpu.VMEM((1, width), F32),
        ],
        compiler_params=pltpu.CompilerParams(dimension_semantics=("arbitrary", "arbitrary"),
                                             vmem_limit_bytes=VMEM_LIMIT_BYTES),
        name="lru_branch",
    )(x2, ys, *weights)


def _ffn_kernel(h_ref, n2_ref, wfi_ref, wfo_ref, nf_ref, out_ref, *, final_norm):
    h = h_ref[...]
    hn = _rms(h, n2_ref[...]).astype(BF16)
    hidden = wfo_ref.shape[0]
    acc = h
    for lo in range(0, hidden, COL_TILE):
        hi = min(lo + COL_TILE, hidden)
        act = (_silu(_dot(hn, wfi_ref[:, lo:hi]))
               * _dot(hn, wfi_ref[:, hidden + lo:hidden + hi])).astype(BF16)
        acc = acc + _dot(act, wfo_ref[lo:hi, :])
    if final_norm:
        acc = _rms(acc, nf_ref[...])
    out_ref[...] = acc


def _ffn(h2, n2, wfi, wfo, nf, final_norm):
    t, d = h2.shape
    tm = FFN_ROWS
    row = lambda i: (i, 0)
    weights = (n2, wfi, wfo, nf)
    return pl.pallas_call(
        functools.partial(_ffn_kernel, final_norm=final_norm),
        grid=(t // tm,),
        in_specs=[pl.BlockSpec((tm, d), row)] + [_resident(w.shape) for w in weights],
        out_specs=pl.BlockSpec((tm, d), row),
        out_shape=jax.ShapeDtypeStruct((t, d), F32),
        compiler_params=pltpu.CompilerParams(dimension_semantics=("arbitrary",),
                                             vmem_limit_bytes=VMEM_LIMIT_BYTES),
        name="ffn",
    )(h2, *weights)


def _cast_main_kernel(wt_ref, out_ref):
    out_ref[...] = wt_ref[...].T.astype(BF16)


def _cast_tail_kernel(wt_ref, wdt_ref, wlx_ref, wly_ref, *, n_dt, n_lx):
    dt = wt_ref[0:n_dt, :]
    dt = jnp.concatenate([dt, jnp.zeros((LANES - n_dt, dt.shape[1]), F32)], axis=0)
    wdt_ref[...] = dt.T.astype(BF16)
    wlx_ref[...] = wt_ref[n_dt:n_dt + n_lx, :].T.astype(BF16)
    wly_ref[...] = wt_ref[n_dt + n_lx:, :].T.astype(BF16)


def _split_w_in(wt, o_z, o_dt, o_lx, o_ly):
    n, d = wt.shape
    tn = 8 * LANES
    assert o_z % tn == 0 and o_dt % tn == 0
    assert o_lx - o_dt <= LANES and o_lx % SUBLANES == 0 and o_ly % SUBLANES == 0
    params = pltpu.CompilerParams(dimension_semantics=("arbitrary",), vmem_limit_bytes=VMEM_LIMIT_BYTES)

    def cast_rows(start, stop, name):
        first = start // tn
        return pl.pallas_call(
            _cast_main_kernel,
            grid=((stop - start) // tn,),
            in_specs=[pl.BlockSpec((tn, d), lambda i: (first + i, 0))],
            out_specs=pl.BlockSpec((d, tn), lambda i: (0, i)),
            out_shape=jax.ShapeDtypeStruct((d, stop - start), BF16),
            compiler_params=params,
            name=name,
        )(wt)

    wg = cast_rows(0, o_z, "cast_w_gates")
    wzx = cast_rows(o_z, o_dt, "cast_w_zxbc")
    widths = (LANES, o_ly - o_lx, n - o_ly)
    wdt, wlx, wly = pl.pallas_call(
        functools.partial(_cast_tail_kernel, n_dt=o_lx - o_dt, n_lx=o_ly - o_lx),
        grid=(1,),
        in_specs=[pl.BlockSpec((pl.Element(n - o_dt), pl.Element(d)), lambda i: (o_dt, 0))],
        out_specs=[pl.BlockSpec((d, w), lambda i: (0, 0)) for w in widths],
        out_shape=[jax.ShapeDtypeStruct((d, w), BF16) for w in widths],
        compiler_params=params,
        name="cast_w_in_tail",
    )(wt)
    return wg, wzx, wdt, wlx, wly


def _pad_lanes(a):
    pad = (-a.shape[-1]) % LANES
    return jnp.pad(a, [(0, 0)] * (a.ndim - 1) + [(0, pad)])


def kernel(x, norm1_w, w_in, b_branch_gate, ssm_conv_w, ssm_conv_b, ssm_dt_bias, ssm_a_log, ssm_d, ssm_norm_w, w_out_ssm, lru_conv_w, lru_conv_b, lru_w_r, lru_b_r, lru_w_i, lru_b_i, lru_lambda, w_out_lru, w_out, norm2_w, w_ffn_in, w_ffn_out, norm_f_w):
    batch, seq, d = x.shape
    depth = norm1_w.shape[0]
    d_inner = w_out_ssm.shape[1]
    conv_dim = ssm_conv_w.shape[2]
    heads = ssm_dt_bias.shape[1]
    width = w_out_lru.shape[1]
    hidden = w_ffn_out.shape[1]
    n_gates = b_branch_gate.shape[1]
    assert seq % STEP_TOKENS == 0 and (batch * seq) % FFN_ROWS == 0
    assert d_inner == heads * SSM_HEAD_DIM and n_gates == 2 * d
    assert d % MIX_TILE == 0 and conv_dim % MIX_TILE == 0 and width % (2 * LANES) == 0
    assert (d_inner // SSM_GROUPS) % LANES == 0 and d_inner % SIDE_TILE == 0 and hidden % LANES == 0

    o_z = n_gates
    o_xbc = o_z + d_inner
    o_dt = o_xbc + conv_dim
    o_lx = o_dt + heads
    o_ly = o_lx + width

    h2 = x.reshape(batch * seq, d)
    for l in range(depth):
        wg, wzx, wdt, wlx, wly = _split_w_in(jnp.swapaxes(w_in, 1, 2)[l], o_z, o_dt, o_lx, o_ly)
        row2 = lambda a: a[l].reshape(1, -1)
        ys = _ssd_branch(h2, batch, seq, (
            row2(norm1_w), wzx, wdt, ssm_conv_w[l], row2(ssm_conv_b), _pad_lanes(row2(ssm_dt_bias)),
            _pad_lanes(row2(ssm_a_log)), jnp.repeat(ssm_d[l], SSM_HEAD_DIM).reshape(1, -1),
            row2(ssm_norm_w), w_out_ssm[l].astype(BF16)))
        h2 = _lru_branch(h2, ys, batch, seq, (
            row2(norm1_w), wg, row2(b_branch_gate), wlx, wly, lru_conv_w[l], row2(lru_conv_b),
            jnp.concatenate([lru_w_r[l], lru_w_i[l]], axis=-1).astype(BF16),
            row2(lru_b_r), row2(lru_b_i), row2(lru_lambda),
            w_out_lru[l].astype(BF16), w_out[l].astype(BF16)))
        h2 = _ffn(h2, row2(norm2_w), w_ffn_in[l].astype(BF16), w_ffn_out[l].astype(BF16),
                  norm_f_w.reshape(1, -1), final_norm=(l == depth - 1))
    return h2.reshape(batch, seq, d)
```

```python
import functools

import jax
import jax.numpy as jnp
from jax import lax
from jax.experimental import pallas as pl
from jax.experimental.pallas import tpu as pltpu

F32 = jnp.float32
BF16 = jnp.bfloat16

SSM_GROUPS = 4
SSM_HEAD_DIM = 64
LRU_C = 8.0
RMS_EPS = 1e-6
LOG2_E = 1.4426950408889634

LANES = 128
SUBLANES = 8
VMEM_LIMIT_BYTES = 56 * 1024 * 1024

PERM_BLOCK = 128
GROUP = PERM_BLOCK // SUBLANES
STEP_TOKENS = 256
STEP_BLOCKS = STEP_TOKENS // PERM_BLOCK
FFN_ROWS = 1024
COL_TILE = 512
MIX_TILE = 256
SIDE_TILE = 256
CAST_ROWS = 1024


def _resident(shape):
    nd = len(shape)
    return pl.BlockSpec(shape, lambda *_: (0,) * nd, pipeline_mode=pl.Buffered(1))


def _softplus(x):
    return jnp.maximum(x, 0.0) + jnp.log1p(jnp.exp(-jnp.abs(x)))


def _sigmoid(x):
    return 0.5 * jnp.tanh(0.5 * x) + 0.5


def _silu(x):
    hx = 0.5 * x
    return hx * jnp.tanh(hx) + hx


def _gelu_tanh(x):
    c0 = 0.7978845608028654
    hx = 0.5 * x
    return hx * jnp.tanh(x * (c0 + (c0 * 0.044715) * (x * x))) + hx


def _sqrt_unit(x):
    return jnp.where(x > 0.0, x * lax.rsqrt(x), 0.0)


def _rms(x, w):
    ms = jnp.mean(x * x, axis=-1, keepdims=True)
    return x * lax.rsqrt(ms + RMS_EPS) * w


def _split3_bf16(v):
    hi = v.astype(BF16)
    r = v - hi.astype(F32)
    mid = r.astype(BF16)
    lo = (r - mid.astype(F32)).astype(BF16)
    return hi, mid, lo


def _dot(a, b):
    return jnp.dot(a, b, preferred_element_type=F32)


def _strided_rows(blk, r):
    return pl.ds(blk * PERM_BLOCK + r, GROUP, stride=SUBLANES)


def _permute_rows(slab_ref, value):
    rows, cols = value.shape
    out = []
    for s in range(cols // LANES):
        slab_ref[s] = value[:, s * LANES:(s + 1) * LANES]
        out.append(jnp.concatenate(
            [slab_ref[s, _strided_rows(blk, r), :]
             for blk in range(rows // PERM_BLOCK) for r in range(SUBLANES)], axis=0))
    return jnp.concatenate(out, axis=1)


def _unpermute_rows(slab_ref, value):
    rows, cols = value.shape
    out = []
    for s in range(cols // LANES):
        for blk in range(rows // PERM_BLOCK):
            for r in range(SUBLANES):
                p = blk * PERM_BLOCK + r * GROUP
                slab_ref[s, _strided_rows(blk, r), :] = value[p:p + GROUP, s * LANES:(s + 1) * LANES]
        out.append(slab_ref[s])
    return jnp.concatenate(out, axis=1)


def _shift_rows(v, s, fill):
    n = v.shape[0]
    if s % SUBLANES == 0:
        head = jnp.broadcast_to(jnp.asarray(fill, v.dtype), (s, v.shape[1]))
        return jnp.concatenate([head, v[:n - s]], axis=0)
    t = lax.broadcasted_iota(jnp.int32, v.shape, 0)
    return jnp.where(t >= s, pltpu.roll(v, s, axis=0), fill)


def _perm_conv(cur, tail, cw, cb):
    k_width = cw.shape[0]
    slabs = [cur[GROUP * r:GROUP * (r + 1)] for r in range(SUBLANES)]
    prev_group = {}
    for d in range(1, k_width):
        q = SUBLANES - d
        prev_group[q] = _shift_rows(slabs[q], 1, tail[k_width - 1 - d:k_width - d, :])
    outs = []
    for r in range(SUBLANES):
        acc = cb
        for k in range(k_width):
            d = k_width - 1 - k
            src = slabs[r - d] if r >= d else prev_group[r - d + SUBLANES]
            acc = acc + cw[k:k + 1, :] * src
        outs.append(acc)
    return jnp.concatenate(outs, axis=0)


def _conv_tail(cur, k_width):
    rows = [GROUP * (SUBLANES - d) + GROUP - 1 for d in range(k_width - 1, 0, -1)]
    return jnp.concatenate([cur[p:p + 1] for p in rows], axis=0)


def _perm_scan(a, b, h_prev):
    pa = [a[0:GROUP]]
    pb = [b[0:GROUP]]
    for r in range(1, SUBLANES):
        ar = a[GROUP * r:GROUP * (r + 1)]
        pa.append(ar * pa[-1])
        pb.append(ar * pb[-1] + b[GROUP * r:GROUP * (r + 1)])
    ga, gb = pa[-1], pb[-1]
    s = 1
    while s < GROUP:
        gb = ga * _shift_rows(gb, s, 0.0) + gb
        ga = ga * _shift_rows(ga, s, 1.0)
        s *= 2
    h_end = ga * h_prev + gb
    h_in = _shift_rows(h_end, 1, h_prev)
    h = jnp.concatenate([pa[r] * h_in + pb[r] for r in range(SUBLANES)], axis=0)
    return h, h_end[GROUP - 1:GROUP]


def _causal_mask():
    lc = PERM_BLOCK
    ri = lax.broadcasted_iota(jnp.int32, (lc, lc), 0)
    ci = lax.broadcasted_iota(jnp.int32, (lc, lc), 1)
    time_of = lambda p: SUBLANES * (p & (GROUP - 1)) + (p >> (GROUP.bit_length() - 1))
    return time_of(ri) >= time_of(ci)


def _ssd_decays(dt_raw, dtb, a_log, heads, causal, cs_ref, rows_ref):
    lc = PERM_BLOCK
    dt = _softplus(dt_raw + dtb)
    d_a = dt * (-jnp.exp(a_log))
    tri = causal.astype(BF16)
    hi, mid, lo = _split3_bf16(d_a)
    cs = (_dot(tri, hi) + _dot(tri, mid) + _dot(tri, lo)) * LOG2_E
    cs_ref[...] = cs
    cs_t = cs.T[:heads]
    dt_t = dt.T[:heads]
    rows_ref[0] = cs_t - jnp.log2(dt_t)
    rows_ref[1] = dt_t * jnp.exp2(cs_t[:, lc - 1:lc] - cs_t)


def _ssd_group(act_ref, rows, g, d_inner, n_state):
    b_off = d_inner + g * n_state
    c_off = d_inner + SSM_GROUPS * n_state + g * n_state
    bm = act_ref[rows, b_off:b_off + n_state]
    cm = act_ref[rows, c_off:c_off + n_state]
    cb = lax.dot_general(cm.astype(BF16), bm.astype(BF16), (((1,), (1,)), ((), ())),
                         preferred_element_type=F32)
    return cm, cb, bm.T


def _ssd_pair(act_ref, rows, j, group, causal, first_half, dsk_ref, st_ref, y_ref, cs_ref, rows_ref):
    lc = PERM_BLOCK
    cm, cb, bm_t = group
    cols = slice(j * LANES, (j + 1) * LANES)
    xs = act_ref[rows, cols]
    s_old = st_ref[j]
    xs_b = xs.astype(BF16)
    rhs = jnp.concatenate([xs_b, s_old.astype(BF16)], axis=0)
    ys, upds, tots = [], [], []
    for k in range(2):
        h = 2 * j + k
        col = jnp.broadcast_to(cs_ref[:, h:h + 1], (lc, lc))
        row = rows_ref[0, h:h + 1, :]
        seg = jnp.exp2(jnp.where(causal, col - row, -jnp.inf))
        g_mat = (cb * seg).astype(BF16)
        c_mat = (cm * jnp.exp2(col)).astype(BF16)
        ys.append(_dot(jnp.concatenate([g_mat, c_mat], axis=1), rhs))
        bw = (bm_t * rows_ref[1, h:h + 1, :]).astype(BF16)
        upds.append(_dot(bw, xs_b))
        tots.append(jnp.exp2(col[lc - 1:lc, :]))
    y_ref[rows, cols] = xs * dsk_ref[:, cols] + jnp.where(first_half, ys[0], ys[1])
    st_ref[j] = (s_old * jnp.where(first_half, tots[0], tots[1])
                 + jnp.where(first_half, upds[0], upds[1]))


def _mixer_kernel(x_ref, nw1_ref, wm_ref, bg_ref, wdt_ref, wlx_ref, wly_ref,
                  cw_ref, cb_ref, dtb_ref, alog_ref, dsk_ref, snw_ref,
                  lcw_ref, lcb_ref, wri_ref, br_ref, bi_ref, lam_ref,
                  wos_ref, wol_ref, wo_ref,
                  out_ref,
                  slab_ref, hn_ref, act_ref, y_ref, yn_ref, lr_ref, mg_ref, lx_ref, gy_ref,
                  zs_ref, g_ref, ctail_ref, st_ref, cs_ref, rows_ref, ltail_ref, h_ref, sp_ref):
    d = x_ref.shape[1]
    d_inner = snw_ref.shape[1]
    conv_dim = cw_ref.shape[1]
    o_z = bg_ref.shape[1]
    o_xbc = o_z + d_inner
    n_state = (conv_dim - d_inner) // (2 * SSM_GROUPS)
    heads = d_inner // SSM_HEAD_DIM
    pairs_per_group = heads // SSM_GROUPS // 2
    width = wlx_ref.shape[1]
    k_ssm = cw_ref.shape[0]
    k_lru = lcw_ref.shape[0]
    blocks = [slice(b * PERM_BLOCK, (b + 1) * PERM_BLOCK) for b in range(STEP_BLOCKS)]

    @pl.when(pl.program_id(1) == 0)
    def _():
        ctail_ref[...] = jnp.zeros(ctail_ref.shape, F32)
        st_ref[...] = jnp.zeros(st_ref.shape, F32)
        ltail_ref[...] = jnp.zeros(ltail_ref.shape, F32)
        h_ref[...] = jnp.zeros(h_ref.shape, F32)
        sp_ref[...] = _softplus(-lam_ref[...])

    hn_ref[...] = _permute_rows(slab_ref, _rms(x_ref[...], nw1_ref[...])).astype(BF16)

    def project_lru(lo):
        pcols = slice(lo, lo + 2 * LANES)
        lx_ref[:, pcols] = _dot(hn_ref[...], wlx_ref[:, pcols])
        gy_ref[:, pcols] = _gelu_tanh(_dot(hn_ref[...], wly_ref[:, pcols]))

    def conv_xbc(lo):
        cols = slice(lo, lo + MIX_TILE)
        proj = _dot(hn_ref[...], wm_ref[:, o_xbc + lo:o_xbc + lo + MIX_TILE])
        for rows in blocks:
            cur = proj[rows]
            act_ref[rows, cols] = _silu(_perm_conv(cur, ctail_ref[:, cols], cw_ref[:, cols], cb_ref[:, cols]))
            ctail_ref[:, cols] = _conv_tail(cur, k_ssm)

    def lru_block(j):
        cols = slice(j * LANES, (j + 1) * LANES)
        u_blocks = []
        for rows in blocks:
            cur = lx_ref[rows, cols]
            u_blocks.append(_perm_conv(cur, ltail_ref[:, cols], lcw_ref[:, cols], lcb_ref[:, cols]))
            ltail_ref[:, cols] = _conv_tail(cur, k_lru)
        u = jnp.concatenate(u_blocks, axis=0)
        ri = _dot(u.astype(BF16), wri_ref[j])
        r_gate = _sigmoid(ri[:, :LANES] + br_ref[:, cols])
        i_gate = _sigmoid(ri[:, LANES:] + bi_ref[:, cols])
        log_a = (-LRU_C) * r_gate * sp_ref[:, cols]
        a = jnp.exp(log_a)
        b_in = _sqrt_unit(1.0 - jnp.exp(2.0 * log_a)) * (i_gate * u)
        for rows in blocks:
            h, h_last = _perm_scan(a[rows], b_in[rows], h_ref[:, cols])
            h_ref[:, cols] = h_last
            lr_ref[rows, cols] = (h * gy_ref[rows, cols]).astype(BF16)

    lru_slices = list(range(0, width, 2 * LANES))
    xbc_slices = list(range(0, conv_dim, MIX_TILE))
    n_lru = width // LANES
    project_lru(lru_slices[0])
    for i in range(max(n_lru, len(xbc_slices))):
        if i % 2 == 0 and i // 2 + 1 < len(lru_slices):
            project_lru(lru_slices[i // 2 + 1])
        if i < len(xbc_slices):
            conv_xbc(xbc_slices[i])
        if i < n_lru:
            lru_block(i)

    def side_z(lo):
        cols = slice(lo, lo + SIDE_TILE)
        zs_ref[:, cols] = _silu(_dot(hn_ref[...], wm_ref[:, o_z + lo:o_z + lo + SIDE_TILE]))

    def side_g(lo):
        cols = slice(lo, lo + SIDE_TILE)
        g_ref[:, cols] = _sigmoid(_dot(hn_ref[...], wm_ref[:, cols]) + bg_ref[:, cols])

    side = ([functools.partial(side_z, lo) for lo in range(0, d_inner, SIDE_TILE)]
            + [functools.partial(side_g, lo) for lo in range(0, 2 * d, SIDE_TILE)])
    n_pairs = STEP_BLOCKS * SSM_GROUPS * pairs_per_group
    dt_raw = _dot(hn_ref[...], wdt_ref[...])
    pairs_done = 0
    side_done = 0
    causal = _causal_mask()
    first_half = lax.broadcasted_iota(jnp.int32, (1, LANES), 1) < SSM_HEAD_DIM
    for b, rows in enumerate(blocks):
        _ssd_decays(dt_raw[rows], dtb_ref[...], alog_ref[...], heads, causal, cs_ref.at[b],
                    rows_ref.at[b])
        for g in range(SSM_GROUPS):
            group = _ssd_group(act_ref, rows, g, d_inner, n_state)
            for pair in range(pairs_per_group):
                _ssd_pair(act_ref, rows, g * pairs_per_group + pair, group, causal, first_half,
                          dsk_ref, st_ref, y_ref, cs_ref.at[b], rows_ref.at[b])
                pairs_done += 1
                while side_done * n_pairs < pairs_done * len(side):
                    side[side_done]()
                    side_done += 1

    gw = d_inner // SSM_GROUPS
    for g in range(SSM_GROUPS):
        cols = slice(g * gw, (g + 1) * gw)
        yg = y_ref[:, cols] * zs_ref[:, cols]
        yn_ref[:, cols] = _rms(yg, snw_ref[:, cols]).astype(BF16)

    for lo in range(0, d, MIX_TILE):
        cols = slice(lo, lo + MIX_TILE)
        gcols = slice(d + lo, d + lo + MIX_TILE)
        merged = (g_ref[:, cols] * _dot(yn_ref[...], wos_ref[:, cols])
                  + g_ref[:, gcols] * _dot(lr_ref[...], wol_ref[:, cols]))
        mg_ref[:, cols] = merged.astype(BF16)
    mixed = _dot(mg_ref[...], wo_ref[...])
    out_ref[...] = x_ref[...] + _unpermute_rows(slab_ref, mixed)


def _mixer(x2, batch, seq, weights):
    t, d = x2.shape
    tm = STEP_TOKENS
    nc = seq // tm
    (nw1, wm, bg, wdt, wlx, wly, cw, cb, dtb, alog, dsk, snw,
     lcw, lcb, wri, br, bi, lam, wos, wol, wo) = weights
    d_inner = snw.shape[1]
    conv_dim = cw.shape[1]
    n_state = (conv_dim - d_inner) // (2 * SSM_GROUPS)
    heads = d_inner // SSM_HEAD_DIM
    width = wlx.shape[1]
    row = lambda b, c: (b * nc + c, 0)
    return pl.pallas_call(
        _mixer_kernel,
        grid=(batch, nc),
        in_specs=[pl.BlockSpec((tm, d), row)] + [_resident(w.shape) for w in weights],
        out_specs=pl.BlockSpec((tm, d), row),
        out_shape=jax.ShapeDtypeStruct((t, d), F32),
        scratch_shapes=[
            pltpu.VMEM((d // LANES, tm, LANES), F32),
            pltpu.VMEM((tm, d), BF16),
            pltpu.VMEM((tm, conv_dim + LANES), F32),
            pltpu.VMEM((tm, d_inner + LANES), F32),
            pltpu.VMEM((tm, d_inner), BF16),
            pltpu.VMEM((tm, width), BF16),
            pltpu.VMEM((tm, d), BF16),
            pltpu.VMEM((tm, width + LANES), F32),
            pltpu.VMEM((tm, width + LANES), F32),
            pltpu.VMEM((tm, d_inner + LANES), F32),
            pltpu.VMEM((tm, 2 * d + LANES), F32),
            pltpu.VMEM((cw.shape[0] - 1, conv_dim), F32),
            pltpu.VMEM((d_inner // LANES, n_state, LANES), F32),
            pltpu.VMEM((STEP_BLOCKS, PERM_BLOCK, LANES), F32),
            pltpu.VMEM((STEP_BLOCKS, 2, heads, PERM_BLOCK), F32),
            pltpu.VMEM((lcw.shape[0] - 1, width), F32),
            pltpu.VMEM((1, width), F32),
            pltpu.VMEM((1, width), F32),
        ],
        compiler_params=pltpu.CompilerParams(dimension_semantics=("arbitrary", "arbitrary"),
                                             vmem_limit_bytes=VMEM_LIMIT_BYTES),
        name="mixer",
    )(x2, *weights)


def _ffn_kernel(h_ref, n2_ref, wfi_ref, wfo_ref, nf_ref, out_ref, *, final_norm):
    h = h_ref[...]
    hn = _rms(h, n2_ref[...]).astype(BF16)
    hidden = wfo_ref.shape[0]
    acc = h
    for lo in range(0, hidden, COL_TILE):
        hi = min(lo + COL_TILE, hidden)
        act = (_silu(_dot(hn, wfi_ref[:, lo:hi]))
               * _dot(hn, wfi_ref[:, hidden + lo:hidden + hi])).astype(BF16)
        acc = acc + _dot(act, wfo_ref[lo:hi, :])
    if final_norm:
        acc = _rms(acc, nf_ref[...])
    out_ref[...] = acc


def _ffn(h2, n2, wfi, wfo, nf, final_norm):
    t, d = h2.shape
    tm = FFN_ROWS
    row = lambda i: (i, 0)
    weights = (n2, wfi, wfo, nf)
    return pl.pallas_call(
        functools.partial(_ffn_kernel, final_norm=final_norm),
        grid=(t // tm,),
        in_specs=[pl.BlockSpec((tm, d), row)] + [_resident(w.shape) for w in weights],
        out_specs=pl.BlockSpec((tm, d), row),
        out_shape=jax.ShapeDtypeStruct((t, d), F32),
        compiler_params=pltpu.CompilerParams(dimension_semantics=("arbitrary",),
                                             vmem_limit_bytes=VMEM_LIMIT_BYTES),
        name="ffn",
    )(h2, *weights)


def _cast_main_kernel(wt_ref, out_ref):
    out_ref[...] = wt_ref[...].T.astype(BF16)


def _cast_tail_kernel(wt_ref, wdt_ref, wlx_ref, wly_ref, *, n_dt, n_lx):
    dt = wt_ref[0:n_dt, :]
    dt = jnp.concatenate([dt, jnp.zeros((LANES - n_dt, dt.shape[1]), F32)], axis=0)
    wdt_ref[...] = dt.T.astype(BF16)
    wlx_ref[...] = wt_ref[n_dt:n_dt + n_lx, :].T.astype(BF16)
    wly_ref[...] = wt_ref[n_dt + n_lx:, :].T.astype(BF16)


def _split_w_in(wt, o_dt, o_lx, o_ly):
    n, d = wt.shape
    tn = CAST_ROWS
    assert o_dt % tn == 0 and o_lx - o_dt <= LANES and o_lx % SUBLANES == 0 and o_ly % SUBLANES == 0
    params = pltpu.CompilerParams(dimension_semantics=("arbitrary",), vmem_limit_bytes=VMEM_LIMIT_BYTES)
    wm = pl.pallas_call(
        _cast_main_kernel,
        grid=(o_dt // tn,),
        in_specs=[pl.BlockSpec((tn, d), lambda i: (i, 0))],
        out_specs=pl.BlockSpec((d, tn), lambda i: (0, i)),
        out_shape=jax.ShapeDtypeStruct((d, o_dt), BF16),
        compiler_params=params,
        name="cast_w_in_main",
    )(wt)
    widths = (LANES, o_ly - o_lx, n - o_ly)
    wdt, wlx, wly = pl.pallas_call(
        functools.partial(_cast_tail_kernel, n_dt=o_lx - o_dt, n_lx=o_ly - o_lx),
        grid=(1,),
        in_specs=[pl.BlockSpec((pl.Element(n - o_dt), pl.Element(d)), lambda i: (o_dt, 0))],
        out_specs=[pl.BlockSpec((d, w), lambda i: (0, 0)) for w in widths],
        out_shape=[jax.ShapeDtypeStruct((d, w), BF16) for w in widths],
        compiler_params=params,
        name="cast_w_in_tail",
    )(wt)
    return wm, wdt, wlx, wly


def _pad_lanes(a):
    pad = (-a.shape[-1]) % LANES
    return jnp.pad(a, [(0, 0)] * (a.ndim - 1) + [(0, pad)])


def kernel(x, norm1_w, w_in, b_branch_gate, ssm_conv_w, ssm_conv_b, ssm_dt_bias, ssm_a_log, ssm_d, ssm_norm_w, w_out_ssm, lru_conv_w, lru_conv_b, lru_w_r, lru_b_r, lru_w_i, lru_b_i, lru_lambda, w_out_lru, w_out, norm2_w, w_ffn_in, w_ffn_out, norm_f_w):
    batch, seq, d = x.shape
    depth = norm1_w.shape[0]
    d_inner = w_out_ssm.shape[1]
    conv_dim = ssm_conv_w.shape[2]
    heads = ssm_dt_bias.shape[1]
    width = w_out_lru.shape[1]
    hidden = w_ffn_out.shape[1]
    n_gates = b_branch_gate.shape[1]
    assert seq % STEP_TOKENS == 0 and (batch * seq) % FFN_ROWS == 0
    assert d_inner == heads * SSM_HEAD_DIM and n_gates == 2 * d
    assert d % MIX_TILE == 0 and conv_dim % MIX_TILE == 0 and width % (2 * LANES) == 0
    assert (d_inner // SSM_GROUPS) % LANES == 0 and d_inner % SIDE_TILE == 0 and hidden % LANES == 0

    o_z = n_gates
    o_xbc = o_z + d_inner
    o_dt = o_xbc + conv_dim
    o_lx = o_dt + heads
    o_ly = o_lx + width

    h2 = x.reshape(batch * seq, d)
    for l in range(depth):
        wm, wdt, wlx, wly = _split_w_in(jnp.swapaxes(w_in, 1, 2)[l], o_dt, o_lx, o_ly)
        row2 = lambda a: a[l].reshape(1, -1)
        weights = (
            row2(norm1_w), wm, row2(b_branch_gate), wdt, wlx, wly,
            ssm_conv_w[l], row2(ssm_conv_b), _pad_lanes(row2(ssm_dt_bias)), _pad_lanes(row2(ssm_a_log)),
            jnp.repeat(ssm_d[l], SSM_HEAD_DIM).reshape(1, -1), row2(ssm_norm_w),
            lru_conv_w[l], row2(lru_conv_b),
            jnp.concatenate([lru_w_r[l], lru_w_i[l]], axis=-1).astype(BF16),
            row2(lru_b_r), row2(lru_b_i), row2(lru_lambda),
            w_out_ssm[l].astype(BF16), w_out_lru[l].astype(BF16), w_out[l].astype(BF16))
        h2 = _mixer(h2, batch, seq, weights)
        h2 = _ffn(h2, row2(norm2_w), w_ffn_in[l].astype(BF16), w_ffn_out[l].astype(BF16),
                  norm_f_w.reshape(1, -1), final_norm=(l == depth - 1))
    return h2.reshape(batch, seq, d)
```

```python
import functools

import jax
import jax.numpy as jnp
from jax import lax
from jax.experimental import pallas as pl
from jax.experimental.pallas import tpu as pltpu

F32 = jnp.float32
BF16 = jnp.bfloat16

SSM_GROUPS = 4
SSM_HEAD_DIM = 64
LRU_C = 8.0
RMS_EPS = 1e-6
LOG2_E = 1.4426950408889634

LANES = 128
SUBLANES = 8
VMEM_LIMIT_BYTES = 56 * 1024 * 1024

PERM_BLOCK = 128
GROUP = PERM_BLOCK // SUBLANES
STEP_TOKENS = 256
STEP_BLOCKS = STEP_TOKENS // PERM_BLOCK
FFN_ROWS = 1024
COL_TILE = 512
MIX_TILE = 256
SIDE_TILE = 256
CAST_ROWS = 1024


def _resident(shape):
    nd = len(shape)
    return pl.BlockSpec(shape, lambda *_: (0,) * nd, pipeline_mode=pl.Buffered(1))


def _softplus(x):
    return jnp.maximum(x, 0.0) + jnp.log1p(jnp.exp(-jnp.abs(x)))


def _sigmoid(x):
    return 0.5 * jnp.tanh(0.5 * x) + 0.5


def _silu(x):
    hx = 0.5 * x
    return hx * jnp.tanh(hx) + hx


def _gelu_tanh(x):
    c0 = 0.7978845608028654
    hx = 0.5 * x
    return hx * jnp.tanh(x * (c0 + (c0 * 0.044715) * (x * x))) + hx


def _sqrt_unit(x):
    return jnp.where(x > 0.0, x * lax.rsqrt(x), 0.0)


def _rms(x, w):
    ms = jnp.mean(x * x, axis=-1, keepdims=True)
    return x * lax.rsqrt(ms + RMS_EPS) * w


def _split3_bf16(v):
    hi = v.astype(BF16)
    r = v - hi.astype(F32)
    mid = r.astype(BF16)
    lo = (r - mid.astype(F32)).astype(BF16)
    return hi, mid, lo


def _dot(a, b):
    return jnp.dot(a, b, preferred_element_type=F32)


def _strided_rows(blk, r):
    return pl.ds(blk * PERM_BLOCK + r, GROUP, stride=SUBLANES)


def _permute_rows(slab_ref, value):
    rows, cols = value.shape
    out = []
    for s in range(cols // LANES):
        slab_ref[s] = value[:, s * LANES:(s + 1) * LANES]
        out.append(jnp.concatenate(
            [slab_ref[s, _strided_rows(blk, r), :]
             for blk in range(rows // PERM_BLOCK) for r in range(SUBLANES)], axis=0))
    return jnp.concatenate(out, axis=1)


def _unpermute_rows(slab_ref, value):
    rows, cols = value.shape
    out = []
    for s in range(cols // LANES):
        for blk in range(rows // PERM_BLOCK):
            for r in range(SUBLANES):
                p = blk * PERM_BLOCK + r * GROUP
                slab_ref[s, _strided_rows(blk, r), :] = value[p:p + GROUP, s * LANES:(s + 1) * LANES]
        out.append(slab_ref[s])
    return jnp.concatenate(out, axis=1)


def _shift_rows(v, s, fill):
    n = v.shape[0]
    if s % SUBLANES == 0:
        head = jnp.broadcast_to(jnp.asarray(fill, v.dtype), (s, v.shape[1]))
        return jnp.concatenate([head, v[:n - s]], axis=0)
    t = lax.broadcasted_iota(jnp.int32, v.shape, 0)
    return jnp.where(t >= s, pltpu.roll(v, s, axis=0), fill)


def _perm_conv(cur, tail, cw, cb):
    k_width = cw.shape[0]
    slabs = [cur[GROUP * r:GROUP * (r + 1)] for r in range(SUBLANES)]
    prev_group = {}
    for d in range(1, k_width):
        q = SUBLANES - d
        prev_group[q] = _shift_rows(slabs[q], 1, tail[k_width - 1 - d:k_width - d, :])
    outs = []
    for r in range(SUBLANES):
        acc = cb
        for k in range(k_width):
            d = k_width - 1 - k
            src = slabs[r - d] if r >= d else prev_group[r - d + SUBLANES]
            acc = acc + cw[k:k + 1, :] * src
        outs.append(acc)
    return jnp.concatenate(outs, axis=0)


def _conv_tail(cur, k_width):
    rows = [GROUP * (SUBLANES - d) + GROUP - 1 for d in range(k_width - 1, 0, -1)]
    return jnp.concatenate([cur[p:p + 1] for p in rows], axis=0)


def _perm_scan(a, b, h_prev):
    pa = [a[0:GROUP]]
    pb = [b[0:GROUP]]
    for r in range(1, SUBLANES):
        ar = a[GROUP * r:GROUP * (r + 1)]
        pa.append(ar * pa[-1])
        pb.append(ar * pb[-1] + b[GROUP * r:GROUP * (r + 1)])
    ga, gb = pa[-1], pb[-1]
    s = 1
    while s < GROUP:
        gb = ga * _shift_rows(gb, s, 0.0) + gb
        ga = ga * _shift_rows(ga, s, 1.0)
        s *= 2
    h_end = ga * h_prev + gb
    h_in = _shift_rows(h_end, 1, h_prev)
    h = jnp.concatenate([pa[r] * h_in + pb[r] for r in range(SUBLANES)], axis=0)
    return h, h_end[GROUP - 1:GROUP]


def _causal_mask():
    lc = PERM_BLOCK
    ri = lax.broadcasted_iota(jnp.int32, (lc, lc), 0)
    ci = lax.broadcasted_iota(jnp.int32, (lc, lc), 1)
    time_of = lambda p: SUBLANES * (p & (GROUP - 1)) + (p >> (GROUP.bit_length() - 1))
    return time_of(ri) >= time_of(ci)


def _ssd_decays(dt_raw, dtb, a_log, heads, causal, cs_ref, rows_ref):
    lc = PERM_BLOCK
    dt = _softplus(dt_raw + dtb)
    d_a = dt * (-jnp.exp(a_log))
    tri = causal.astype(BF16)
    hi, mid, lo = _split3_bf16(d_a)
    cs = (_dot(tri, hi) + _dot(tri, mid) + _dot(tri, lo)) * LOG2_E
    cs_ref[...] = cs
    cs_t = cs.T[:heads]
    dt_t = dt.T[:heads]
    rows_ref[0] = cs_t - jnp.log2(dt_t)
    rows_ref[1] = dt_t * jnp.exp2(cs_t[:, lc - 1:lc] - cs_t)


def _ssd_group(act_ref, rows, g, d_inner, n_state):
    b_off = d_inner + g * n_state
    c_off = d_inner + SSM_GROUPS * n_state + g * n_state
    bm = act_ref[rows, b_off:b_off + n_state]
    cm = act_ref[rows, c_off:c_off + n_state]
    cb = lax.dot_general(cm.astype(BF16), bm.astype(BF16), (((1,), (1,)), ((), ())),
                         preferred_element_type=F32)
    return cm, cb, bm.T


def _ssd_pair(act_ref, rows, j, group, causal, first_half, dsk_ref, st_ref, y_ref, cs_ref, rows_ref):
    lc = PERM_BLOCK
    cm, cb, bm_t = group
    cols = slice(j * LANES, (j + 1) * LANES)
    xs = act_ref[rows, cols]
    s_old = st_ref[j]
    xs_b = xs.astype(BF16)
    rhs = jnp.concatenate([xs_b, s_old.astype(BF16)], axis=0)
    ys, upds, tots = [], [], []
    for k in range(2):
        h = 2 * j + k
        col = jnp.broadcast_to(cs_ref[:, h:h + 1], (lc, lc))
        row = rows_ref[0, h:h + 1, :]
        seg = jnp.exp2(jnp.where(causal, col - row, -jnp.inf))
        g_mat = (cb * seg).astype(BF16)
        c_mat = (cm * jnp.exp2(col)).astype(BF16)
        ys.append(_dot(jnp.concatenate([g_mat, c_mat], axis=1), rhs))
        bw = (bm_t * rows_ref[1, h:h + 1, :]).astype(BF16)
        upds.append(_dot(bw, xs_b))
        tots.append(jnp.exp2(col[lc - 1:lc, :]))
    y_ref[rows, cols] = xs * dsk_ref[:, cols] + jnp.where(first_half, ys[0], ys[1])
    st_ref[j] = (s_old * jnp.where(first_half, tots[0], tots[1])
                 + jnp.where(first_half, upds[0], upds[1]))


def _mixer_kernel(x_ref, nw1_ref, wm_ref, bg_ref, wdt_ref, wlx_ref, wly_ref,
                  cw_ref, cb_ref, dtb_ref, alog_ref, dsk_ref, snw_ref,
                  lcw_ref, lcb_ref, wri_ref, br_ref, bi_ref, lam_ref,
                  wos_ref, wol_ref, wo_ref,
                  out_ref,
                  slab_ref, hn_ref, act_ref, y_ref, yn_ref, lr_ref, mg_ref, lx_ref, gy_ref,
                  zs_ref, g_ref, ctail_ref, st_ref, cs_ref, rows_ref, ltail_ref, h_ref, sp_ref):
    d = x_ref.shape[1]
    d_inner = snw_ref.shape[1]
    conv_dim = cw_ref.shape[1]
    o_z = bg_ref.shape[1]
    o_xbc = o_z + d_inner
    n_state = (conv_dim - d_inner) // (2 * SSM_GROUPS)
    heads = d_inner // SSM_HEAD_DIM
    pairs_per_group = heads // SSM_GROUPS // 2
    width = wlx_ref.shape[1]
    k_ssm = cw_ref.shape[0]
    k_lru = lcw_ref.shape[0]
    blocks = [slice(b * PERM_BLOCK, (b + 1) * PERM_BLOCK) for b in range(STEP_BLOCKS)]

    @pl.when(pl.program_id(1) == 0)
    def _():
        ctail_ref[...] = jnp.zeros(ctail_ref.shape, F32)
        st_ref[...] = jnp.zeros(st_ref.shape, F32)
        ltail_ref[...] = jnp.zeros(ltail_ref.shape, F32)
        h_ref[...] = jnp.zeros(h_ref.shape, F32)
        sp_ref[...] = _softplus(-lam_ref[...])

    hn_ref[...] = _permute_rows(slab_ref, _rms(x_ref[...], nw1_ref[...])).astype(BF16)

    def project_lru(lo):
        pcols = slice(lo, lo + 2 * LANES)
        lx_ref[:, pcols] = _dot(hn_ref[...], wlx_ref[:, pcols])
        gy_ref[:, pcols] = _gelu_tanh(_dot(hn_ref[...], wly_ref[:, pcols]))

    def conv_xbc(lo):
        cols = slice(lo, lo + MIX_TILE)
        proj = _dot(hn_ref[...], wm_ref[:, o_xbc + lo:o_xbc + lo + MIX_TILE])
        for rows in blocks:
            cur = proj[rows]
            act_ref[rows, cols] = _silu(_perm_conv(cur, ctail_ref[:, cols], cw_ref[:, cols], cb_ref[:, cols]))
            ctail_ref[:, cols] = _conv_tail(cur, k_ssm)

    def lru_block(j):
        cols = slice(j * LANES, (j + 1) * LANES)
        u_blocks = []
        for rows in blocks:
            cur = lx_ref[rows, cols]
            u_blocks.append(_perm_conv(cur, ltail_ref[:, cols], lcw_ref[:, cols], lcb_ref[:, cols]))
            ltail_ref[:, cols] = _conv_tail(cur, k_lru)
        u = jnp.concatenate(u_blocks, axis=0)
        ri = _dot(u.astype(BF16), wri_ref[j])
        r_gate = _sigmoid(ri[:, :LANES] + br_ref[:, cols])
        i_gate = _sigmoid(ri[:, LANES:] + bi_ref[:, cols])
        log_a = (-LRU_C) * r_gate * sp_ref[:, cols]
        a = jnp.exp(log_a)
        b_in = _sqrt_unit(1.0 - jnp.exp(2.0 * log_a)) * (i_gate * u)
        for rows in blocks:
            h, h_last = _perm_scan(a[rows], b_in[rows], h_ref[:, cols])
            h_ref[:, cols] = h_last
            lr_ref[rows, cols] = (h * gy_ref[rows, cols]).astype(BF16)

    lru_slices = list(range(0, width, 2 * LANES))
    xbc_slices = list(range(0, conv_dim, MIX_TILE))
    n_lru = width // LANES
    project_lru(lru_slices[0])
    for i in range(max(n_lru, len(xbc_slices))):
        if i % 2 == 0 and i // 2 + 1 < len(lru_slices):
            project_lru(lru_slices[i // 2 + 1])
        if i < len(xbc_slices):
            conv_xbc(xbc_slices[i])
        if i < n_lru:
            lru_block(i)

    def side_z(lo):
        cols = slice(lo, lo + SIDE_TILE)
        zs_ref[:, cols] = _silu(_dot(hn_ref[...], wm_ref[:, o_z + lo:o_z + lo + SIDE_TILE]))

    def side_g(lo):
        cols = slice(lo, lo + SIDE_TILE)
        g_ref[:, cols] = _sigmoid(_dot(hn_ref[...], wm_ref[:, cols]) + bg_ref[:, cols])

    side = ([functools.partial(side_z, lo) for lo in range(0, d_inner, SIDE_TILE)]
            + [functools.partial(side_g, lo) for lo in range(0, 2 * d, SIDE_TILE)])
    n_pairs = STEP_BLOCKS * SSM_GROUPS * pairs_per_group
    dt_raw = _dot(hn_ref[...], wdt_ref[...])
    pairs_done = 0
    side_done = 0
    causal = _causal_mask()
    first_half = lax.broadcasted_iota(jnp.int32, (1, LANES), 1) < SSM_HEAD_DIM
    for b, rows in enumerate(blocks):
        _ssd_decays(dt_raw[rows], dtb_ref[...], alog_ref[...], heads, causal, cs_ref.at[b],
                    rows_ref.at[b])
        for g in range(SSM_GROUPS):
            group = _ssd_group(act_ref, rows, g, d_inner, n_state)
            for pair in range(pairs_per_group):
                _ssd_pair(act_ref, rows, g * pairs_per_group + pair, group, causal, first_half,
                          dsk_ref, st_ref, y_ref, cs_ref.at[b], rows_ref.at[b])
                pairs_done += 1
                while side_done * n_pairs < pairs_done * len(side):
                    side[side_done]()
                    side_done += 1

    gw = d_inner // SSM_GROUPS
    for g in range(SSM_GROUPS):
        cols = slice(g * gw, (g + 1) * gw)
        yg = y_ref[:, cols] * zs_ref[:, cols]
        yn_ref[:, cols] = _rms(yg, snw_ref[:, cols]).astype(BF16)

    for lo in range(0, d, MIX_TILE):
        cols = slice(lo, lo + MIX_TILE)
        gcols = slice(d + lo, d + lo + MIX_TILE)
        merged = (g_ref[:, cols] * _dot(yn_ref[...], wos_ref[:, cols])
                  + g_ref[:, gcols] * _dot(lr_ref[...], wol_ref[:, cols]))
        mg_ref[:, cols] = merged.astype(BF16)
    mixed = _dot(mg_ref[...], wo_ref[...])
    out_ref[...] = x_ref[...] + _unpermute_rows(slab_ref, mixed)


def _mixer(x2, batch, seq, weights):
    t, d = x2.shape
    tm = STEP_TOKENS
    nc = seq // tm
    (nw1, wm, bg, wdt, wlx, wly, cw, cb, dtb, alog, dsk, snw,
     lcw, lcb, wri, br, bi, lam, wos, wol, wo) = weights
    d_inner = snw.shape[1]
    conv_dim = cw.shape[1]
    n_state = (conv_dim - d_inner) // (2 * SSM_GROUPS)
    heads = d_inner // SSM_HEAD_DIM
    width = wlx.shape[1]
    row = lambda b, c: (b * nc + c, 0)
    return pl.pallas_call(
        _mixer_kernel,
        grid=(batch, nc),
        in_specs=[pl.BlockSpec((tm, d), row)] + [_resident(w.shape) for w in weights],
        out_specs=pl.BlockSpec((tm, d), row),
        out_shape=jax.ShapeDtypeStruct((t, d), F32),
        scratch_shapes=[
            pltpu.VMEM((d // LANES, tm, LANES), F32),
            pltpu.VMEM((tm, d), BF16),
            pltpu.VMEM((tm, conv_dim), F32),
            pltpu.VMEM((tm, d_inner), F32),
            pltpu.VMEM((tm, d_inner), BF16),
            pltpu.VMEM((tm, width), BF16),
            pltpu.VMEM((tm, d), BF16),
            pltpu.VMEM((tm, width), F32),
            pltpu.VMEM((tm, width), F32),
            pltpu.VMEM((tm, d_inner), F32),
            pltpu.VMEM((tm, 2 * d), F32),
            pltpu.VMEM((cw.shape[0] - 1, conv_dim), F32),
            pltpu.VMEM((d_inner // LANES, n_state, LANES), F32),
            pltpu.VMEM((STEP_BLOCKS, PERM_BLOCK, LANES), F32),
            pltpu.VMEM((STEP_BLOCKS, 2, heads, PERM_BLOCK), F32),
            pltpu.VMEM((lcw.shape[0] - 1, width), F32),
            pltpu.VMEM((1, width), F32),
            pltpu.VMEM((1, width), F32),
        ],
        compiler_params=pltpu.CompilerParams(dimension_semantics=("arbitrary", "arbitrary"),
                                             vmem_limit_bytes=VMEM_LIMIT_BYTES),
        name="mixer",
    )(x2, *weights)


def _ffn_kernel(h_ref, n2_ref, wfi_ref, wfo_ref, nf_ref, out_ref, *, final_norm):
    h = h_ref[...]
    hn = _rms(h, n2_ref[...]).astype(BF16)
    hidden = wfo_ref.shape[0]
    acc = h
    for lo in range(0, hidden, COL_TILE):
        hi = min(lo + COL_TILE, hidden)
        act = (_silu(_dot(hn, wfi_ref[:, lo:hi]))
               * _dot(hn, wfi_ref[:, hidden + lo:hidden + hi])).astype(BF16)
        acc = acc + _dot(act, wfo_ref[lo:hi, :])
    if final_norm:
        acc = _rms(acc, nf_ref[...])
    out_ref[...] = acc


def _ffn(h2, n2, wfi, wfo, nf, final_norm):
    t, d = h2.shape
    tm = FFN_ROWS
    row = lambda i: (i, 0)
    weights = (n2, wfi, wfo, nf)
    return pl.pallas_call(
        functools.partial(_ffn_kernel, final_norm=final_norm),
        grid=(t // tm,),
        in_specs=[pl.BlockSpec((tm, d), row)] + [_resident(w.shape) for w in weights],
        out_specs=pl.BlockSpec((tm, d), row),
        out_shape=jax.ShapeDtypeStruct((t, d), F32),
        compiler_params=pltpu.CompilerParams(dimension_semantics=("arbitrary",),
                                             vmem_limit_bytes=VMEM_LIMIT_BYTES),
        name="ffn",
    )(h2, *weights)


def _cast_main_kernel(wt_ref, out_ref):
    out_ref[...] = wt_ref[...].T.astype(BF16)


def _cast_tail_kernel(wt_ref, wdt_ref, wlx_ref, wly_ref, *, n_dt, n_lx):
    dt = wt_ref[0:n_dt, :]
    dt = jnp.concatenate([dt, jnp.zeros((LANES - n_dt, dt.shape[1]), F32)], axis=0)
    wdt_ref[...] = dt.T.astype(BF16)
    wlx_ref[...] = wt_ref[n_dt:n_dt + n_lx, :].T.astype(BF16)
    wly_ref[...] = wt_ref[n_dt + n_lx:, :].T.astype(BF16)


def _split_w_in(wt, o_dt, o_lx, o_ly):
    n, d = wt.shape
    tn = CAST_ROWS
    assert o_dt % tn == 0 and o_lx - o_dt <= LANES and o_lx % SUBLANES == 0 and o_ly % SUBLANES == 0
    params = pltpu.CompilerParams(dimension_semantics=("arbitrary",), vmem_limit_bytes=VMEM_LIMIT_BYTES)
    wm = pl.pallas_call(
        _cast_main_kernel,
        grid=(o_dt // tn,),
        in_specs=[pl.BlockSpec((tn, d), lambda i: (i, 0))],
        out_specs=pl.BlockSpec((d, tn), lambda i: (0, i)),
        out_shape=jax.ShapeDtypeStruct((d, o_dt), BF16),
        compiler_params=params,
        name="cast_w_in_main",
    )(wt)
    widths = (LANES, o_ly - o_lx, n - o_ly)
    wdt, wlx, wly = pl.pallas_call(
        functools.partial(_cast_tail_kernel, n_dt=o_lx - o_dt, n_lx=o_ly - o_lx),
        grid=(1,),
        in_specs=[pl.BlockSpec((pl.Element(n - o_dt), pl.Element(d)), lambda i: (o_dt, 0))],
        out_specs=[pl.BlockSpec((d, w), lambda i: (0, 0)) for w in widths],
        out_shape=[jax.ShapeDtypeStruct((d, w), BF16) for w in widths],
        compiler_params=params,
        name="cast_w_in_tail",
    )(wt)
    return wm, wdt, wlx, wly


def _pad_lanes(a):
    pad = (-a.shape[-1]) % LANES
    return jnp.pad(a, [(0, 0)] * (a.ndim - 1) + [(0, pad)])


def kernel(x, norm1_w, w_in, b_branch_gate, ssm_conv_w, ssm_conv_b, ssm_dt_bias, ssm_a_log, ssm_d, ssm_norm_w, w_out_ssm, lru_conv_w, lru_conv_b, lru_w_r, lru_b_r, lru_w_i, lru_b_i, lru_lambda, w_out_lru, w_out, norm2_w, w_ffn_in, w_ffn_out, norm_f_w):
    batch, seq, d = x.shape
    depth = norm1_w.shape[0]
    d_inner = w_out_ssm.shape[1]
    conv_dim = ssm_conv_w.shape[2]
    heads = ssm_dt_bias.shape[1]
    width = w_out_lru.shape[1]
    hidden = w_ffn_out.shape[1]
    n_gates = b_branch_gate.shape[1]
    assert seq % STEP_TOKENS == 0 and (batch * seq) % FFN_ROWS == 0
    assert d_inner == heads * SSM_HEAD_DIM and n_gates == 2 * d
    assert d % MIX_TILE == 0 and conv_dim % MIX_TILE == 0 and width % (2 * LANES) == 0
    assert (d_inner // SSM_GROUPS) % LANES == 0 and d_inner % SIDE_TILE == 0 and hidden % LANES == 0

    o_z = n_gates
    o_xbc = o_z + d_inner
    o_dt = o_xbc + conv_dim
    o_lx = o_dt + heads
    o_ly = o_lx + width

    h2 = x.reshape(batch * seq, d)
    for l in range(depth):
        wm, wdt, wlx, wly = _split_w_in(jnp.swapaxes(w_in, 1, 2)[l], o_dt, o_lx, o_ly)
        row2 = lambda a: a[l].reshape(1, -1)
        weights = (
            row2(norm1_w), wm, row2(b_branch_gate), wdt, wlx, wly,
            ssm_conv_w[l], row2(ssm_conv_b), _pad_lanes(row2(ssm_dt_bias)), _pad_lanes(row2(ssm_a_log)),
            jnp.repeat(ssm_d[l], SSM_HEAD_DIM).reshape(1, -1), row2(ssm_norm_w),
            lru_conv_w[l], row2(lru_conv_b),
            jnp.concatenate([lru_w_r[l], lru_w_i[l]], axis=-1).astype(BF16),
            row2(lru_b_r), row2(lru_b_i), row2(lru_lambda),
            w_out_ssm[l].astype(BF16), w_out_lru[l].astype(BF16), w_out[l].astype(BF16))
        h2 = _mixer(h2, batch, seq, weights)
        h2 = _ffn(h2, row2(norm2_w), w_ffn_in[l].astype(BF16), w_ffn_out[l].astype(BF16),
                  norm_f_w.reshape(1, -1), final_norm=(l == depth - 1))
    return h2.reshape(batch, seq, d)
```

```python
import functools

import jax
import jax.numpy as jnp
from jax import lax
from jax.experimental import pallas as pl
from jax.experimental.pallas import tpu as pltpu

F32 = jnp.float32
BF16 = jnp.bfloat16

SSM_GROUPS = 4
SSM_HEAD_DIM = 64
LRU_C = 8.0
RMS_EPS = 1e-6
LOG2_E = 1.4426950408889634

LANES = 128
SUBLANES = 8
VMEM_LIMIT_BYTES = 56 * 1024 * 1024

PERM_BLOCK = 128
GROUP = PERM_BLOCK // SUBLANES
STEP_TOKENS = 256
STEP_BLOCKS = STEP_TOKENS // PERM_BLOCK
FFN_ROWS = 1024
COL_TILE = 512
MIX_TILE = 256
SIDE_TILE = 256
CAST_ROWS = 1024


def _resident(shape):
    nd = len(shape)
    return pl.BlockSpec(shape, lambda *_: (0,) * nd, pipeline_mode=pl.Buffered(1))


def _softplus(x):
    return jnp.maximum(x, 0.0) + jnp.log1p(jnp.exp(-jnp.abs(x)))


def _sigmoid(x):
    return 0.5 * jnp.tanh(0.5 * x) + 0.5


def _silu(x):
    hx = 0.5 * x
    return hx * jnp.tanh(hx) + hx


def _gelu_tanh(x):
    c0 = 0.7978845608028654
    hx = 0.5 * x
    return hx * jnp.tanh(x * (c0 + (c0 * 0.044715) * (x * x))) + hx


def _sqrt_unit(x):
    return jnp.where(x > 0.0, x * lax.rsqrt(x), 0.0)


def _rms(x, w):
    ms = jnp.mean(x * x, axis=-1, keepdims=True)
    return x * lax.rsqrt(ms + RMS_EPS) * w


def _split3_bf16(v):
    hi = v.astype(BF16)
    r = v - hi.astype(F32)
    mid = r.astype(BF16)
    lo = (r - mid.astype(F32)).astype(BF16)
    return hi, mid, lo


def _dot(a, b):
    return jnp.dot(a, b, preferred_element_type=F32)


def _strided_rows(blk, r):
    return pl.ds(blk * PERM_BLOCK + r, GROUP, stride=SUBLANES)


def _permute_rows(slab_ref, value):
    rows, cols = value.shape
    out = []
    for s in range(cols // LANES):
        slab_ref[s] = value[:, s * LANES:(s + 1) * LANES]
        out.append(jnp.concatenate(
            [slab_ref[s, _strided_rows(blk, r), :]
             for blk in range(rows // PERM_BLOCK) for r in range(SUBLANES)], axis=0))
    return jnp.concatenate(out, axis=1)


def _unpermute_rows(slab_ref, value):
    rows, cols = value.shape
    out = []
    for s in range(cols // LANES):
        for blk in range(rows // PERM_BLOCK):
            for r in range(SUBLANES):
                p = blk * PERM_BLOCK + r * GROUP
                slab_ref[s, _strided_rows(blk, r), :] = value[p:p + GROUP, s * LANES:(s + 1) * LANES]
        out.append(slab_ref[s])
    return jnp.concatenate(out, axis=1)


def _shift_rows(v, s, fill):
    n = v.shape[0]
    if s % SUBLANES == 0:
        head = jnp.broadcast_to(jnp.asarray(fill, v.dtype), (s, v.shape[1]))
        return jnp.concatenate([head, v[:n - s]], axis=0)
    t = lax.broadcasted_iota(jnp.int32, v.shape, 0)
    return jnp.where(t >= s, pltpu.roll(v, s, axis=0), fill)


def _perm_conv(cur, tail, cw, cb):
    k_width = cw.shape[0]
    slabs = [cur[GROUP * r:GROUP * (r + 1)] for r in range(SUBLANES)]
    prev_group = {}
    for d in range(1, k_width):
        q = SUBLANES - d
        prev_group[q] = _shift_rows(slabs[q], 1, tail[k_width - 1 - d:k_width - d, :])
    outs = []
    for r in range(SUBLANES):
        acc = cb
        for k in range(k_width):
            d = k_width - 1 - k
            src = slabs[r - d] if r >= d else prev_group[r - d + SUBLANES]
            acc = acc + cw[k:k + 1, :] * src
        outs.append(acc)
    return jnp.concatenate(outs, axis=0)


def _conv_tail(cur, k_width):
    rows = [GROUP * (SUBLANES - d) + GROUP - 1 for d in range(k_width - 1, 0, -1)]
    return jnp.concatenate([cur[p:p + 1] for p in rows], axis=0)


def _perm_scan(a, b, h_prev):
    pa = [a[0:GROUP]]
    pb = [b[0:GROUP]]
    for r in range(1, SUBLANES):
        ar = a[GROUP * r:GROUP * (r + 1)]
        pa.append(ar * pa[-1])
        pb.append(ar * pb[-1] + b[GROUP * r:GROUP * (r + 1)])
    ga, gb = pa[-1], pb[-1]
    s = 1
    while s < GROUP:
        gb = ga * _shift_rows(gb, s, 0.0) + gb
        ga = ga * _shift_rows(ga, s, 1.0)
        s *= 2
    h_end = ga * h_prev + gb
    h_in = _shift_rows(h_end, 1, h_prev)
    h = jnp.concatenate([pa[r] * h_in + pb[r] for r in range(SUBLANES)], axis=0)
    return h, h_end[GROUP - 1:GROUP]


def _causal_mask():
    lc = PERM_BLOCK
    ri = lax.broadcasted_iota(jnp.int32, (lc, lc), 0)
    ci = lax.broadcasted_iota(jnp.int32, (lc, lc), 1)
    time_of = lambda p: SUBLANES * (p & (GROUP - 1)) + (p >> (GROUP.bit_length() - 1))
    return time_of(ri) >= time_of(ci)


def _ssd_decays(dt_raw, dtb, a_log, heads, causal, cs_ref, rows_ref):
    lc = PERM_BLOCK
    dt = _softplus(dt_raw + dtb)
    d_a = dt * (-jnp.exp(a_log))
    tri = causal.astype(BF16)
    hi, mid, lo = _split3_bf16(d_a)
    cs = (_dot(tri, hi) + _dot(tri, mid) + _dot(tri, lo)) * LOG2_E
    cs_ref[...] = cs
    cs_t = cs.T[:heads]
    dt_t = dt.T[:heads]
    rows_ref[0] = cs_t - jnp.log2(dt_t)
    rows_ref[1] = dt_t * jnp.exp2(cs_t[:, lc - 1:lc] - cs_t)


def _ssd_group(act_ref, rows, g, d_inner, n_state):
    b_off = d_inner + g * n_state
    c_off = d_inner + SSM_GROUPS * n_state + g * n_state
    bm = act_ref[rows, b_off:b_off + n_state]
    cm = act_ref[rows, c_off:c_off + n_state]
    cb = lax.dot_general(cm.astype(BF16), bm.astype(BF16), (((1,), (1,)), ((), ())),
                         preferred_element_type=F32)
    return cm, cb, bm.T


def _ssd_pair(act_ref, rows, j, group, causal, first_half, dsk_ref, st_ref, y_ref, cs_ref, rows_ref):
    lc = PERM_BLOCK
    cm, cb, bm_t = group
    cols = slice(j * LANES, (j + 1) * LANES)
    xs = act_ref[rows, cols]
    s_old = st_ref[j]
    xs_b = xs.astype(BF16)
    rhs = jnp.concatenate([xs_b, s_old.astype(BF16)], axis=0)
    ys, upds, tots = [], [], []
    for k in range(2):
        h = 2 * j + k
        col = jnp.broadcast_to(cs_ref[:, h:h + 1], (lc, lc))
        row = rows_ref[0, h:h + 1, :]
        seg = jnp.exp2(jnp.where(causal, col - row, -jnp.inf))
        g_mat = (cb * seg).astype(BF16)
        c_mat = (cm * jnp.exp2(col)).astype(BF16)
        ys.append(_dot(jnp.concatenate([g_mat, c_mat], axis=1), rhs))
        bw = (bm_t * rows_ref[1, h:h + 1, :]).astype(BF16)
        upds.append(_dot(bw, xs_b))
        tots.append(jnp.exp2(col[lc - 1:lc, :]))
    y_ref[rows, cols] = xs * dsk_ref[:, cols] + jnp.where(first_half, ys[0], ys[1])
    st_ref[j] = (s_old * jnp.where(first_half, tots[0], tots[1])
                 + jnp.where(first_half, upds[0], upds[1]))


def _mixer_kernel(x_ref, xnext_ref, nw1_ref, wm_ref, bg_ref, wdt_ref, wlx_ref, wly_ref,
                  cw_ref, cb_ref, dtb_ref, alog_ref, dsk_ref, snw_ref,
                  lcw_ref, lcb_ref, wri_ref, br_ref, bi_ref, lam_ref,
                  wos_ref, wol_ref, wo_ref,
                  out_ref,
                  slab_ref, nslab_ref, hn2_ref, act_ref, y_ref, yn_ref, lr_ref, mg_ref, lx_ref, gy_ref,
                  zs_ref, g_ref, ctail_ref, st_ref, cs_ref, rows_ref, ltail_ref, h_ref, sp_ref):
    d = x_ref.shape[1]
    d_inner = snw_ref.shape[1]
    conv_dim = cw_ref.shape[1]
    o_z = bg_ref.shape[1]
    o_xbc = o_z + d_inner
    n_state = (conv_dim - d_inner) // (2 * SSM_GROUPS)
    heads = d_inner // SSM_HEAD_DIM
    pairs_per_group = heads // SSM_GROUPS // 2
    width = wlx_ref.shape[1]
    k_ssm = cw_ref.shape[0]
    k_lru = lcw_ref.shape[0]
    blocks = [slice(b * PERM_BLOCK, (b + 1) * PERM_BLOCK) for b in range(STEP_BLOCKS)]

    @pl.when(pl.program_id(1) == 0)
    def _():
        ctail_ref[...] = jnp.zeros(ctail_ref.shape, F32)
        st_ref[...] = jnp.zeros(st_ref.shape, F32)
        ltail_ref[...] = jnp.zeros(ltail_ref.shape, F32)
        h_ref[...] = jnp.zeros(h_ref.shape, F32)
        sp_ref[...] = _softplus(-lam_ref[...])

    flat = pl.program_id(0) * pl.num_programs(1) + pl.program_id(1)
    slot = lax.rem(flat, 2)

    def normed_input(src_ref):
        return _permute_rows(nslab_ref, _rms(src_ref[...], nw1_ref[...])).astype(BF16)

    @pl.when(flat == 0)
    def _():
        hn2_ref[0] = normed_input(x_ref)

    hn_ref = hn2_ref.at[slot]

    def project_lru(lo):
        pcols = slice(lo, lo + 2 * LANES)
        lx_ref[:, pcols] = _dot(hn_ref[...], wlx_ref[:, pcols])
        gy_ref[:, pcols] = _gelu_tanh(_dot(hn_ref[...], wly_ref[:, pcols]))

    def conv_xbc(lo):
        cols = slice(lo, lo + MIX_TILE)
        proj = _dot(hn_ref[...], wm_ref[:, o_xbc + lo:o_xbc + lo + MIX_TILE])
        for rows in blocks:
            cur = proj[rows]
            act_ref[rows, cols] = _silu(_perm_conv(cur, ctail_ref[:, cols], cw_ref[:, cols], cb_ref[:, cols]))
            ctail_ref[:, cols] = _conv_tail(cur, k_ssm)

    def lru_block(j):
        cols = slice(j * LANES, (j + 1) * LANES)
        u_blocks = []
        for rows in blocks:
            cur = lx_ref[rows, cols]
            u_blocks.append(_perm_conv(cur, ltail_ref[:, cols], lcw_ref[:, cols], lcb_ref[:, cols]))
            ltail_ref[:, cols] = _conv_tail(cur, k_lru)
        u = jnp.concatenate(u_blocks, axis=0)
        ri = _dot(u.astype(BF16), wri_ref[j])
        r_gate = _sigmoid(ri[:, :LANES] + br_ref[:, cols])
        i_gate = _sigmoid(ri[:, LANES:] + bi_ref[:, cols])
        log_a = (-LRU_C) * r_gate * sp_ref[:, cols]
        a = jnp.exp(log_a)
        b_in = _sqrt_unit(1.0 - jnp.exp(2.0 * log_a)) * (i_gate * u)
        for rows in blocks:
            h, h_last = _perm_scan(a[rows], b_in[rows], h_ref[:, cols])
            h_ref[:, cols] = h_last
            lr_ref[rows, cols] = (h * gy_ref[rows, cols]).astype(BF16)

    lru_slices = list(range(0, width, 2 * LANES))
    xbc_slices = list(range(0, conv_dim, MIX_TILE))
    n_lru = width // LANES
    project_lru(lru_slices[0])
    for i in range(max(n_lru, len(xbc_slices))):
        if i % 2 == 0 and i // 2 + 1 < len(lru_slices):
            project_lru(lru_slices[i // 2 + 1])
        if i < len(xbc_slices):
            conv_xbc(xbc_slices[i])
        if i < n_lru:
            lru_block(i)

    def side_z(lo):
        cols = slice(lo, lo + SIDE_TILE)
        zs_ref[:, cols] = _silu(_dot(hn_ref[...], wm_ref[:, o_z + lo:o_z + lo + SIDE_TILE]))

    def side_g(lo):
        cols = slice(lo, lo + SIDE_TILE)
        g_ref[:, cols] = _sigmoid(_dot(hn_ref[...], wm_ref[:, cols]) + bg_ref[:, cols])

    side = ([functools.partial(side_z, lo) for lo in range(0, d_inner, SIDE_TILE)]
            + [functools.partial(side_g, lo) for lo in range(0, 2 * d, SIDE_TILE)])
    n_pairs = STEP_BLOCKS * SSM_GROUPS * pairs_per_group
    dt_raw = _dot(hn_ref[...], wdt_ref[...])
    pairs_done = 0
    side_done = 0
    causal = _causal_mask()
    first_half = lax.broadcasted_iota(jnp.int32, (1, LANES), 1) < SSM_HEAD_DIM
    for b, rows in enumerate(blocks):
        _ssd_decays(dt_raw[rows], dtb_ref[...], alog_ref[...], heads, causal, cs_ref.at[b],
                    rows_ref.at[b])
        for g in range(SSM_GROUPS):
            group = _ssd_group(act_ref, rows, g, d_inner, n_state)
            for pair in range(pairs_per_group):
                _ssd_pair(act_ref, rows, g * pairs_per_group + pair, group, causal, first_half,
                          dsk_ref, st_ref, y_ref, cs_ref.at[b], rows_ref.at[b])
                pairs_done += 1
                while side_done * n_pairs < pairs_done * len(side):
                    side[side_done]()
                    side_done += 1

    gw = d_inner // SSM_GROUPS
    for g in range(SSM_GROUPS):
        cols = slice(g * gw, (g + 1) * gw)
        yg = y_ref[:, cols] * zs_ref[:, cols]
        yn_ref[:, cols] = _rms(yg, snw_ref[:, cols]).astype(BF16)

    hn2_ref[1 - slot] = normed_input(xnext_ref)

    for lo in range(0, d, MIX_TILE):
        cols = slice(lo, lo + MIX_TILE)
        gcols = slice(d + lo, d + lo + MIX_TILE)
        merged = (g_ref[:, cols] * _dot(yn_ref[...], wos_ref[:, cols])
                  + g_ref[:, gcols] * _dot(lr_ref[...], wol_ref[:, cols]))
        mg_ref[:, cols] = merged.astype(BF16)
    mixed = _dot(mg_ref[...], wo_ref[...])
    out_ref[...] = x_ref[...] + _unpermute_rows(slab_ref, mixed)


def _mixer(x2, batch, seq, weights):
    t, d = x2.shape
    tm = STEP_TOKENS
    nc = seq // tm
    (nw1, wm, bg, wdt, wlx, wly, cw, cb, dtb, alog, dsk, snw,
     lcw, lcb, wri, br, bi, lam, wos, wol, wo) = weights
    d_inner = snw.shape[1]
    conv_dim = cw.shape[1]
    n_state = (conv_dim - d_inner) // (2 * SSM_GROUPS)
    heads = d_inner // SSM_HEAD_DIM
    width = wlx.shape[1]
    row = lambda b, c: (b * nc + c, 0)
    next_row = lambda b, c: (jnp.minimum(b * nc + c + 1, batch * nc - 1), 0)
    return pl.pallas_call(
        _mixer_kernel,
        grid=(batch, nc),
        in_specs=([pl.BlockSpec((tm, d), row), pl.BlockSpec((tm, d), next_row)]
                  + [_resident(w.shape) for w in weights]),
        out_specs=pl.BlockSpec((tm, d), row),
        out_shape=jax.ShapeDtypeStruct((t, d), F32),
        scratch_shapes=[
            pltpu.VMEM((d // LANES, tm, LANES), F32),
            pltpu.VMEM((d // LANES, tm, LANES), F32),
            pltpu.VMEM((2, tm, d), BF16),
            pltpu.VMEM((tm, conv_dim), F32),
            pltpu.VMEM((tm, d_inner), F32),
            pltpu.VMEM((tm, d_inner), BF16),
            pltpu.VMEM((tm, width), BF16),
            pltpu.VMEM((tm, d), BF16),
            pltpu.VMEM((tm, width), F32),
            pltpu.VMEM((tm, width), F32),
            pltpu.VMEM((tm, d_inner), F32),
            pltpu.VMEM((tm, 2 * d), F32),
            pltpu.VMEM((cw.shape[0] - 1, conv_dim), F32),
            pltpu.VMEM((d_inner // LANES, n_state, LANES), F32),
            pltpu.VMEM((STEP_BLOCKS, PERM_BLOCK, LANES), F32),
            pltpu.VMEM((STEP_BLOCKS, 2, heads, PERM_BLOCK), F32),
            pltpu.VMEM((lcw.shape[0] - 1, width), F32),
            pltpu.VMEM((1, width), F32),
            pltpu.VMEM((1, width), F32),
        ],
        compiler_params=pltpu.CompilerParams(dimension_semantics=("arbitrary", "arbitrary"),
                                             vmem_limit_bytes=VMEM_LIMIT_BYTES),
        name="mixer",
    )(x2, x2, *weights)


def _ffn_kernel(h_ref, n2_ref, wfi_ref, wfo_ref, nf_ref, out_ref, *, final_norm):
    h = h_ref[...]
    hn = _rms(h, n2_ref[...]).astype(BF16)
    hidden = wfo_ref.shape[0]
    acc = h
    for lo in range(0, hidden, COL_TILE):
        hi = min(lo + COL_TILE, hidden)
        act = (_silu(_dot(hn, wfi_ref[:, lo:hi]))
               * _dot(hn, wfi_ref[:, hidden + lo:hidden + hi])).astype(BF16)
        acc = acc + _dot(act, wfo_ref[lo:hi, :])
    if final_norm:
        acc = _rms(acc, nf_ref[...])
    out_ref[...] = acc


def _ffn(h2, n2, wfi, wfo, nf, final_norm):
    t, d = h2.shape
    tm = FFN_ROWS
    row = lambda i: (i, 0)
    weights = (n2, wfi, wfo, nf)
    return pl.pallas_call(
        functools.partial(_ffn_kernel, final_norm=final_norm),
        grid=(t // tm,),
        in_specs=[pl.BlockSpec((tm, d), row)] + [_resident(w.shape) for w in weights],
        out_specs=pl.BlockSpec((tm, d), row),
        out_shape=jax.ShapeDtypeStruct((t, d), F32),
        compiler_params=pltpu.CompilerParams(dimension_semantics=("arbitrary",),
                                             vmem_limit_bytes=VMEM_LIMIT_BYTES),
        name="ffn",
    )(h2, *weights)


def _cast_main_kernel(wt_ref, out_ref):
    out_ref[...] = wt_ref[...].T.astype(BF16)


def _cast_tail_kernel(wt_ref, wdt_ref, wlx_ref, wly_ref, *, n_dt, n_lx):
    dt = wt_ref[0:n_dt, :]
    dt = jnp.concatenate([dt, jnp.zeros((LANES - n_dt, dt.shape[1]), F32)], axis=0)
    wdt_ref[...] = dt.T.astype(BF16)
    wlx_ref[...] = wt_ref[n_dt:n_dt + n_lx, :].T.astype(BF16)
    wly_ref[...] = wt_ref[n_dt + n_lx:, :].T.astype(BF16)


def _split_w_in(wt, o_dt, o_lx, o_ly):
    n, d = wt.shape
    tn = CAST_ROWS
    assert o_dt % tn == 0 and o_lx - o_dt <= LANES and o_lx % SUBLANES == 0 and o_ly % SUBLANES == 0
    params = pltpu.CompilerParams(dimension_semantics=("arbitrary",), vmem_limit_bytes=VMEM_LIMIT_BYTES)
    wm = pl.pallas_call(
        _cast_main_kernel,
        grid=(o_dt // tn,),
        in_specs=[pl.BlockSpec((tn, d), lambda i: (i, 0))],
        out_specs=pl.BlockSpec((d, tn), lambda i: (0, i)),
        out_shape=jax.ShapeDtypeStruct((d, o_dt), BF16),
        compiler_params=params,
        name="cast_w_in_main",
    )(wt)
    widths = (LANES, o_ly - o_lx, n - o_ly)
    wdt, wlx, wly = pl.pallas_call(
        functools.partial(_cast_tail_kernel, n_dt=o_lx - o_dt, n_lx=o_ly - o_lx),
        grid=(1,),
        in_specs=[pl.BlockSpec((pl.Element(n - o_dt), pl.Element(d)), lambda i: (o_dt, 0))],
        out_specs=[pl.BlockSpec((d, w), lambda i: (0, 0)) for w in widths],
        out_shape=[jax.ShapeDtypeStruct((d, w), BF16) for w in widths],
        compiler_params=params,
        name="cast_w_in_tail",
    )(wt)
    return wm, wdt, wlx, wly


def _pad_lanes(a):
    pad = (-a.shape[-1]) % LANES
    return jnp.pad(a, [(0, 0)] * (a.ndim - 1) + [(0, pad)])


def kernel(x, norm1_w, w_in, b_branch_gate, ssm_conv_w, ssm_conv_b, ssm_dt_bias, ssm_a_log, ssm_d, ssm_norm_w, w_out_ssm, lru_conv_w, lru_conv_b, lru_w_r, lru_b_r, lru_w_i, lru_b_i, lru_lambda, w_out_lru, w_out, norm2_w, w_ffn_in, w_ffn_out, norm_f_w):
    batch, seq, d = x.shape
    depth = norm1_w.shape[0]
    d_inner = w_out_ssm.shape[1]
    conv_dim = ssm_conv_w.shape[2]
    heads = ssm_dt_bias.shape[1]
    width = w_out_lru.shape[1]
    hidden = w_ffn_out.shape[1]
    n_gates = b_branch_gate.shape[1]
    assert seq % STEP_TOKENS == 0 and (batch * seq) % FFN_ROWS == 0
    assert d_inner == heads * SSM_HEAD_DIM and n_gates == 2 * d
    assert d % MIX_TILE == 0 and conv_dim % MIX_TILE == 0 and width % (2 * LANES) == 0
    assert (d_inner // SSM_GROUPS) % LANES == 0 and d_inner % SIDE_TILE == 0 and hidden % LANES == 0

    o_z = n_gates
    o_xbc = o_z + d_inner
    o_dt = o_xbc + conv_dim
    o_lx = o_dt + heads
    o_ly = o_lx + width

    h2 = x.reshape(batch * seq, d)
    for l in range(depth):
        wm, wdt, wlx, wly = _split_w_in(jnp.swapaxes(w_in, 1, 2)[l], o_dt, o_lx, o_ly)
        row2 = lambda a: a[l].reshape(1, -1)
        weights = (
            row2(norm1_w), wm, row2(b_branch_gate), wdt, wlx, wly,
            ssm_conv_w[l], row2(ssm_conv_b), _pad_lanes(row2(ssm_dt_bias)), _pad_lanes(row2(ssm_a_log)),
            jnp.repeat(ssm_d[l], SSM_HEAD_DIM).reshape(1, -1), row2(ssm_norm_w),
            lru_conv_w[l], row2(lru_conv_b),
            jnp.concatenate([lru_w_r[l], lru_w_i[l]], axis=-1).astype(BF16),
            row2(lru_b_r), row2(lru_b_i), row2(lru_lambda),
            w_out_ssm[l].astype(BF16), w_out_lru[l].astype(BF16), w_out[l].astype(BF16))
        h2 = _mixer(h2, batch, seq, weights)
        h2 = _ffn(h2, row2(norm2_w), w_ffn_in[l].astype(BF16), w_ffn_out[l].astype(BF16),
                  norm_f_w.reshape(1, -1), final_norm=(l == depth - 1))
    return h2.reshape(batch, seq, d)
```

```python
import functools

import jax
import jax.numpy as jnp
from jax import lax
from jax.experimental import pallas as pl
from jax.experimental.pallas import tpu as pltpu

F32 = jnp.float32
BF16 = jnp.bfloat16

SSM_GROUPS = 4
SSM_HEAD_DIM = 64
LRU_C = 8.0
RMS_EPS = 1e-6
LOG2_E = 1.4426950408889634

LANES = 128
SUBLANES = 8
VMEM_LIMIT_BYTES = 56 * 1024 * 1024

PERM_BLOCK = 128
GROUP = PERM_BLOCK // SUBLANES
STEP_TOKENS = 256
STEP_BLOCKS = STEP_TOKENS // PERM_BLOCK
FFN_ROWS = 1024
COL_TILE = 1024
MIX_TILE = 256
OUT_TILE = 512
SIDE_TILE = 256
CAST_ROWS = 1024


def _resident(shape):
    nd = len(shape)
    return pl.BlockSpec(shape, lambda *_: (0,) * nd, pipeline_mode=pl.Buffered(1))


def _softplus(x):
    return jnp.maximum(x, 0.0) + jnp.log1p(jnp.exp(-jnp.abs(x)))


def _sigmoid(x):
    return 0.5 * jnp.tanh(0.5 * x) + 0.5


def _silu(x):
    hx = 0.5 * x
    return hx * jnp.tanh(hx) + hx


def _gelu_tanh(x):
    c0 = 0.7978845608028654
    hx = 0.5 * x
    return hx * jnp.tanh(x * (c0 + (c0 * 0.044715) * (x * x))) + hx


def _sqrt_unit(x):
    return jnp.where(x > 0.0, x * lax.rsqrt(x), 0.0)


def _rms(x, w):
    ms = jnp.mean(x * x, axis=-1, keepdims=True)
    return x * lax.rsqrt(ms + RMS_EPS) * w


def _split3_bf16(v):
    hi = v.astype(BF16)
    r = v - hi.astype(F32)
    mid = r.astype(BF16)
    lo = (r - mid.astype(F32)).astype(BF16)
    return hi, mid, lo


def _dot(a, b):
    return jnp.dot(a, b, preferred_element_type=F32)


def _strided_rows(blk, r):
    return pl.ds(blk * PERM_BLOCK + r, GROUP, stride=SUBLANES)


def _permute_rows(slab_ref, value):
    rows, cols = value.shape
    out = []
    for s in range(cols // LANES):
        slab_ref[s] = value[:, s * LANES:(s + 1) * LANES]
        out.append(jnp.concatenate(
            [slab_ref[s, _strided_rows(blk, r), :]
             for blk in range(rows // PERM_BLOCK) for r in range(SUBLANES)], axis=0))
    return jnp.concatenate(out, axis=1)


def _unpermute_rows(slab_ref, value):
    rows, cols = value.shape
    out = []
    for s in range(cols // LANES):
        for blk in range(rows // PERM_BLOCK):
            for r in range(SUBLANES):
                p = blk * PERM_BLOCK + r * GROUP
                slab_ref[s, _strided_rows(blk, r), :] = value[p:p + GROUP, s * LANES:(s + 1) * LANES]
        out.append(slab_ref[s])
    return jnp.concatenate(out, axis=1)


def _shift_rows(v, s, fill):
    n = v.shape[0]
    if s % SUBLANES == 0:
        head = jnp.broadcast_to(jnp.asarray(fill, v.dtype), (s, v.shape[1]))
        return jnp.concatenate([head, v[:n - s]], axis=0)
    t = lax.broadcasted_iota(jnp.int32, v.shape, 0)
    return jnp.where(t >= s, pltpu.roll(v, s, axis=0), fill)


def _perm_conv(cur, tail, cw, cb):
    k_width = cw.shape[0]
    slabs = [cur[GROUP * r:GROUP * (r + 1)] for r in range(SUBLANES)]
    prev_group = {}
    for d in range(1, k_width):
        q = SUBLANES - d
        prev_group[q] = _shift_rows(slabs[q], 1, tail[k_width - 1 - d:k_width - d, :])
    outs = []
    for r in range(SUBLANES):
        acc = cb
        for k in range(k_width):
            d = k_width - 1 - k
            src = slabs[r - d] if r >= d else prev_group[r - d + SUBLANES]
            acc = acc + cw[k:k + 1, :] * src
        outs.append(acc)
    return jnp.concatenate(outs, axis=0)


def _conv_tail(cur, k_width):
    rows = [GROUP * (SUBLANES - d) + GROUP - 1 for d in range(k_width - 1, 0, -1)]
    return jnp.concatenate([cur[p:p + 1] for p in rows], axis=0)


def _perm_scan(a, b, h_prev):
    pa = [a[0:GROUP]]
    pb = [b[0:GROUP]]
    for r in range(1, SUBLANES):
        ar = a[GROUP * r:GROUP * (r + 1)]
        pa.append(ar * pa[-1])
        pb.append(ar * pb[-1] + b[GROUP * r:GROUP * (r + 1)])
    ga, gb = pa[-1], pb[-1]
    s = 1
    while s < GROUP:
        gb = ga * _shift_rows(gb, s, 0.0) + gb
        ga = ga * _shift_rows(ga, s, 1.0)
        s *= 2
    h_end = ga * h_prev + gb
    h_in = _shift_rows(h_end, 1, h_prev)
    h = jnp.concatenate([pa[r] * h_in + pb[r] for r in range(SUBLANES)], axis=0)
    return h, h_end[GROUP - 1:GROUP]


def _causal_mask():
    lc = PERM_BLOCK
    ri = lax.broadcasted_iota(jnp.int32, (lc, lc), 0)
    ci = lax.broadcasted_iota(jnp.int32, (lc, lc), 1)
    time_of = lambda p: SUBLANES * (p & (GROUP - 1)) + (p >> (GROUP.bit_length() - 1))
    return time_of(ri) >= time_of(ci)


def _ssd_decays(dt_raw, dtb, a_log, heads, causal, cs_ref, rows_ref):
    lc = PERM_BLOCK
    dt = _softplus(dt_raw + dtb)
    d_a = dt * (-jnp.exp(a_log))
    tri = causal.astype(BF16)
    hi, mid, lo = _split3_bf16(d_a)
    cs = (_dot(tri, hi) + _dot(tri, mid) + _dot(tri, lo)) * LOG2_E
    cs_ref[...] = cs
    cs_t = cs.T[:heads]
    dt_t = dt.T[:heads]
    rows_ref[0] = cs_t - jnp.log2(dt_t)
    rows_ref[1] = dt_t * jnp.exp2(cs_t[:, lc - 1:lc] - cs_t)


def _ssd_group(act_ref, rows, g, d_inner, n_state):
    b_off = d_inner + g * n_state
    c_off = d_inner + SSM_GROUPS * n_state + g * n_state
    bm = act_ref[rows, b_off:b_off + n_state]
    cm = act_ref[rows, c_off:c_off + n_state]
    cb = lax.dot_general(cm.astype(BF16), bm.astype(BF16), (((1,), (1,)), ((), ())),
                         preferred_element_type=F32)
    return cm, cb, bm.T


def _ssd_pair(act_ref, rows, j, group, causal, first_half, dsk_ref, st_ref, y_ref, cs_ref, rows_ref):
    lc = PERM_BLOCK
    cm, cb, bm_t = group
    cols = slice(j * LANES, (j + 1) * LANES)
    xs = act_ref[rows, cols]
    s_old = st_ref[j]
    xs_b = xs.astype(BF16)
    rhs = jnp.concatenate([xs_b, s_old.astype(BF16)], axis=0)
    ys, upds, tots = [], [], []
    for k in range(2):
        h = 2 * j + k
        col = jnp.broadcast_to(cs_ref[:, h:h + 1], (lc, lc))
        row = rows_ref[0, h:h + 1, :]
        seg = jnp.exp2(jnp.where(causal, col - row, -jnp.inf))
        g_mat = (cb * seg).astype(BF16)
        c_mat = (cm * jnp.exp2(col)).astype(BF16)
        ys.append(_dot(jnp.concatenate([g_mat, c_mat], axis=1), rhs))
        bw = (bm_t * rows_ref[1, h:h + 1, :]).astype(BF16)
        upds.append(_dot(bw, xs_b))
        tots.append(jnp.exp2(col[lc - 1:lc, :]))
    y_ref[rows, cols] = xs * dsk_ref[:, cols] + jnp.where(first_half, ys[0], ys[1])
    st_ref[j] = (s_old * jnp.where(first_half, tots[0], tots[1])
                 + jnp.where(first_half, upds[0], upds[1]))


def _mixer_kernel(x_ref, nw1_ref, wm_ref, bg_ref, wdt_ref, wlx_ref, wly_ref,
                  cw_ref, cb_ref, dtb_ref, alog_ref, dsk_ref, snw_ref,
                  lcw_ref, lcb_ref, wri_ref, br_ref, bi_ref, lam_ref,
                  wos_ref, wol_ref, wo_ref,
                  out_ref,
                  slab_ref, hn_ref, act_ref, y_ref, yn_ref, lr_ref, mg_ref, lx_ref, gy_ref,
                  zs_ref, g_ref, ctail_ref, st_ref, cs_ref, rows_ref, ltail_ref, h_ref, sp_ref):
    d = x_ref.shape[1]
    d_inner = snw_ref.shape[1]
    conv_dim = cw_ref.shape[1]
    o_z = bg_ref.shape[1]
    o_xbc = o_z + d_inner
    n_state = (conv_dim - d_inner) // (2 * SSM_GROUPS)
    heads = d_inner // SSM_HEAD_DIM
    pairs_per_group = heads // SSM_GROUPS // 2
    width = wlx_ref.shape[1]
    k_ssm = cw_ref.shape[0]
    k_lru = lcw_ref.shape[0]
    blocks = [slice(b * PERM_BLOCK, (b + 1) * PERM_BLOCK) for b in range(STEP_BLOCKS)]

    @pl.when(pl.program_id(1) == 0)
    def _():
        ctail_ref[...] = jnp.zeros(ctail_ref.shape, F32)
        st_ref[...] = jnp.zeros(st_ref.shape, F32)
        ltail_ref[...] = jnp.zeros(ltail_ref.shape, F32)
        h_ref[...] = jnp.zeros(h_ref.shape, F32)
        sp_ref[...] = _softplus(-lam_ref[...])

    hn_ref[...] = _permute_rows(slab_ref, _rms(x_ref[...], nw1_ref[...])).astype(BF16)

    def project_lru(lo):
        pcols = slice(lo, lo + 2 * LANES)
        lx_ref[:, pcols] = _dot(hn_ref[...], wlx_ref[:, pcols])
        gy_ref[:, pcols] = _gelu_tanh(_dot(hn_ref[...], wly_ref[:, pcols]))

    def conv_xbc(lo):
        cols = slice(lo, lo + MIX_TILE)
        proj = _dot(hn_ref[...], wm_ref[:, o_xbc + lo:o_xbc + lo + MIX_TILE])
        for rows in blocks:
            cur = proj[rows]
            act_ref[rows, cols] = _silu(_perm_conv(cur, ctail_ref[:, cols], cw_ref[:, cols], cb_ref[:, cols]))
            ctail_ref[:, cols] = _conv_tail(cur, k_ssm)

    def lru_block(j):
        cols = slice(j * LANES, (j + 1) * LANES)
        u_blocks = []
        for rows in blocks:
            cur = lx_ref[rows, cols]
            u_blocks.append(_perm_conv(cur, ltail_ref[:, cols], lcw_ref[:, cols], lcb_ref[:, cols]))
            ltail_ref[:, cols] = _conv_tail(cur, k_lru)
        u = jnp.concatenate(u_blocks, axis=0)
        ri = _dot(u.astype(BF16), wri_ref[j])
        r_gate = _sigmoid(ri[:, :LANES] + br_ref[:, cols])
        i_gate = _sigmoid(ri[:, LANES:] + bi_ref[:, cols])
        log_a = (-LRU_C) * r_gate * sp_ref[:, cols]
        a = jnp.exp(log_a)
        b_in = _sqrt_unit(1.0 - jnp.exp(2.0 * log_a)) * (i_gate * u)
        for rows in blocks:
            h, h_last = _perm_scan(a[rows], b_in[rows], h_ref[:, cols])
            h_ref[:, cols] = h_last
            lr_ref[rows, cols] = (h * gy_ref[rows, cols]).astype(BF16)

    lru_slices = list(range(0, width, 2 * LANES))
    xbc_slices = list(range(0, conv_dim, MIX_TILE))
    n_lru = width // LANES
    project_lru(lru_slices[0])
    for i in range(max(n_lru, len(xbc_slices))):
        if i % 2 == 0 and i // 2 + 1 < len(lru_slices):
            project_lru(lru_slices[i // 2 + 1])
        if i < len(xbc_slices):
            conv_xbc(xbc_slices[i])
        if i < n_lru:
            lru_block(i)

    def side_z(lo):
        cols = slice(lo, lo + SIDE_TILE)
        zs_ref[:, cols] = _silu(_dot(hn_ref[...], wm_ref[:, o_z + lo:o_z + lo + SIDE_TILE]))

    def side_g(lo):
        cols = slice(lo, lo + SIDE_TILE)
        g_ref[:, cols] = _sigmoid(_dot(hn_ref[...], wm_ref[:, cols]) + bg_ref[:, cols])

    side = ([functools.partial(side_z, lo) for lo in range(0, d_inner, SIDE_TILE)]
            + [functools.partial(side_g, lo) for lo in range(0, 2 * d, SIDE_TILE)])
    n_pairs = STEP_BLOCKS * SSM_GROUPS * pairs_per_group
    dt_raw = _dot(hn_ref[...], wdt_ref[...])
    pairs_done = 0
    side_done = 0
    causal = _causal_mask()
    first_half = lax.broadcasted_iota(jnp.int32, (1, LANES), 1) < SSM_HEAD_DIM
    for b, rows in enumerate(blocks):
        _ssd_decays(dt_raw[rows], dtb_ref[...], alog_ref[...], heads, causal, cs_ref.at[b],
                    rows_ref.at[b])
        for g in range(SSM_GROUPS):
            group = _ssd_group(act_ref, rows, g, d_inner, n_state)
            for pair in range(pairs_per_group):
                _ssd_pair(act_ref, rows, g * pairs_per_group + pair, group, causal, first_half,
                          dsk_ref, st_ref, y_ref, cs_ref.at[b], rows_ref.at[b])
                pairs_done += 1
                while side_done * n_pairs < pairs_done * len(side):
                    side[side_done]()
                    side_done += 1

    gw = d_inner // SSM_GROUPS
    for g in range(SSM_GROUPS):
        cols = slice(g * gw, (g + 1) * gw)
        yg = y_ref[:, cols] * zs_ref[:, cols]
        yn_ref[:, cols] = _rms(yg, snw_ref[:, cols]).astype(BF16)

    for lo in range(0, d, OUT_TILE):
        cols = slice(lo, lo + OUT_TILE)
        gcols = slice(d + lo, d + lo + OUT_TILE)
        merged = (g_ref[:, cols] * _dot(yn_ref[...], wos_ref[:, cols])
                  + g_ref[:, gcols] * _dot(lr_ref[...], wol_ref[:, cols]))
        mg_ref[:, cols] = merged.astype(BF16)
    mixed = _dot(mg_ref[...], wo_ref[...])
    out_ref[...] = x_ref[...] + _unpermute_rows(slab_ref, mixed)


def _mixer(x2, batch, seq, weights):
    t, d = x2.shape
    tm = STEP_TOKENS
    nc = seq // tm
    (nw1, wm, bg, wdt, wlx, wly, cw, cb, dtb, alog, dsk, snw,
     lcw, lcb, wri, br, bi, lam, wos, wol, wo) = weights
    d_inner = snw.shape[1]
    conv_dim = cw.shape[1]
    n_state = (conv_dim - d_inner) // (2 * SSM_GROUPS)
    heads = d_inner // SSM_HEAD_DIM
    width = wlx.shape[1]
    row = lambda b, c: (b * nc + c, 0)
    return pl.pallas_call(
        _mixer_kernel,
        grid=(batch, nc),
        in_specs=[pl.BlockSpec((tm, d), row)] + [_resident(w.shape) for w in weights],
        out_specs=pl.BlockSpec((tm, d), row),
        out_shape=jax.ShapeDtypeStruct((t, d), F32),
        scratch_shapes=[
            pltpu.VMEM((d // LANES, tm, LANES), F32),
            pltpu.VMEM((tm, d), BF16),
            pltpu.VMEM((tm, conv_dim), F32),
            pltpu.VMEM((tm, d_inner), F32),
            pltpu.VMEM((tm, d_inner), BF16),
            pltpu.VMEM((tm, width), BF16),
            pltpu.VMEM((tm, d), BF16),
            pltpu.VMEM((tm, width), F32),
            pltpu.VMEM((tm, width), F32),
            pltpu.VMEM((tm, d_inner), F32),
            pltpu.VMEM((tm, 2 * d), F32),
            pltpu.VMEM((cw.shape[0] - 1, conv_dim), F32),
            pltpu.VMEM((d_inner // LANES, n_state, LANES), F32),
            pltpu.VMEM((STEP_BLOCKS, PERM_BLOCK, LANES), F32),
            pltpu.VMEM((STEP_BLOCKS, 2, heads, PERM_BLOCK), F32),
            pltpu.VMEM((lcw.shape[0] - 1, width), F32),
            pltpu.VMEM((1, width), F32),
            pltpu.VMEM((1, width), F32),
        ],
        compiler_params=pltpu.CompilerParams(dimension_semantics=("arbitrary", "arbitrary"),
                                             vmem_limit_bytes=VMEM_LIMIT_BYTES),
        name="mixer",
    )(x2, *weights)


def _ffn_kernel(h_ref, n2_ref, wfi_ref, wfo_ref, nf_ref, out_ref, *, final_norm):
    h = h_ref[...]
    hn = _rms(h, n2_ref[...]).astype(BF16)
    hidden = wfo_ref.shape[0]
    acc = h
    for lo in range(0, hidden, COL_TILE):
        hi = min(lo + COL_TILE, hidden)
        act = (_silu(_dot(hn, wfi_ref[:, lo:hi]))
               * _dot(hn, wfi_ref[:, hidden + lo:hidden + hi])).astype(BF16)
        acc = acc + _dot(act, wfo_ref[lo:hi, :])
    if final_norm:
        acc = _rms(acc, nf_ref[...])
    out_ref[...] = acc


def _ffn(h2, n2, wfi, wfo, nf, final_norm):
    t, d = h2.shape
    tm = FFN_ROWS
    row = lambda i: (i, 0)
    weights = (n2, wfi, wfo, nf)
    return pl.pallas_call(
        functools.partial(_ffn_kernel, final_norm=final_norm),
        grid=(t // tm,),
        in_specs=[pl.BlockSpec((tm, d), row)] + [_resident(w.shape) for w in weights],
        out_specs=pl.BlockSpec((tm, d), row),
        out_shape=jax.ShapeDtypeStruct((t, d), F32),
        compiler_params=pltpu.CompilerParams(dimension_semantics=("arbitrary",),
                                             vmem_limit_bytes=VMEM_LIMIT_BYTES),
        name="ffn",
    )(h2, *weights)


def _cast_main_kernel(wt_ref, out_ref):
    out_ref[...] = wt_ref[...].T.astype(BF16)


def _cast_tail_kernel(wt_ref, wdt_ref, wlx_ref, wly_ref, *, n_dt, n_lx):
    dt = wt_ref[0:n_dt, :]
    dt = jnp.concatenate([dt, jnp.zeros((LANES - n_dt, dt.shape[1]), F32)], axis=0)
    wdt_ref[...] = dt.T.astype(BF16)
    wlx_ref[...] = wt_ref[n_dt:n_dt + n_lx, :].T.astype(BF16)
    wly_ref[...] = wt_ref[n_dt + n_lx:, :].T.astype(BF16)


def _split_w_in(wt, o_dt, o_lx, o_ly):
    n, d = wt.shape
    tn = CAST_ROWS
    assert o_dt % tn == 0 and o_lx - o_dt <= LANES and o_lx % SUBLANES == 0 and o_ly % SUBLANES == 0
    params = pltpu.CompilerParams(dimension_semantics=("arbitrary",), vmem_limit_bytes=VMEM_LIMIT_BYTES)
    wm = pl.pallas_call(
        _cast_main_kernel,
        grid=(o_dt // tn,),
        in_specs=[pl.BlockSpec((tn, d), lambda i: (i, 0))],
        out_specs=pl.BlockSpec((d, tn), lambda i: (0, i)),
        out_shape=jax.ShapeDtypeStruct((d, o_dt), BF16),
        compiler_params=params,
        name="cast_w_in_main",
    )(wt)
    widths = (LANES, o_ly - o_lx, n - o_ly)
    wdt, wlx, wly = pl.pallas_call(
        functools.partial(_cast_tail_kernel, n_dt=o_lx - o_dt, n_lx=o_ly - o_lx),
        grid=(1,),
        in_specs=[pl.BlockSpec((pl.Element(n - o_dt), pl.Element(d)), lambda i: (o_dt, 0))],
        out_specs=[pl.BlockSpec((d, w), lambda i: (0, 0)) for w in widths],
        out_shape=[jax.ShapeDtypeStruct((d, w), BF16) for w in widths],
        compiler_params=params,
        name="cast_w_in_tail",
    )(wt)
    return wm, wdt, wlx, wly


def _pad_lanes(a):
    pad = (-a.shape[-1]) % LANES
    return jnp.pad(a, [(0, 0)] * (a.ndim - 1) + [(0, pad)])


def kernel(x, norm1_w, w_in, b_branch_gate, ssm_conv_w, ssm_conv_b, ssm_dt_bias, ssm_a_log, ssm_d, ssm_norm_w, w_out_ssm, lru_conv_w, lru_conv_b, lru_w_r, lru_b_r, lru_w_i, lru_b_i, lru_lambda, w_out_lru, w_out, norm2_w, w_ffn_in, w_ffn_out, norm_f_w):
    batch, seq, d = x.shape
    depth = norm1_w.shape[0]
    d_inner = w_out_ssm.shape[1]
    conv_dim = ssm_conv_w.shape[2]
    heads = ssm_dt_bias.shape[1]
    width = w_out_lru.shape[1]
    hidden = w_ffn_out.shape[1]
    n_gates = b_branch_gate.shape[1]
    assert seq % STEP_TOKENS == 0 and (batch * seq) % FFN_ROWS == 0
    assert d_inner == heads * SSM_HEAD_DIM and n_gates == 2 * d
    assert d % OUT_TILE == 0 and conv_dim % MIX_TILE == 0 and width % (2 * LANES) == 0
    assert (d_inner // SSM_GROUPS) % LANES == 0 and d_inner % SIDE_TILE == 0 and hidden % LANES == 0

    o_z = n_gates
    o_xbc = o_z + d_inner
    o_dt = o_xbc + conv_dim
    o_lx = o_dt + heads
    o_ly = o_lx + width

    h2 = x.reshape(batch * seq, d)
    for l in range(depth):
        wm, wdt, wlx, wly = _split_w_in(jnp.swapaxes(w_in, 1, 2)[l], o_dt, o_lx, o_ly)
        row2 = lambda a: a[l].reshape(1, -1)
        weights = (
            row2(norm1_w), wm, row2(b_branch_gate), wdt, wlx, wly,
            ssm_conv_w[l], row2(ssm_conv_b), _pad_lanes(row2(ssm_dt_bias)), _pad_lanes(row2(ssm_a_log)),
            jnp.repeat(ssm_d[l], SSM_HEAD_DIM).reshape(1, -1), row2(ssm_norm_w),
            lru_conv_w[l], row2(lru_conv_b),
            jnp.concatenate([lru_w_r[l], lru_w_i[l]], axis=-1).astype(BF16),
            row2(lru_b_r), row2(lru_b_i), row2(lru_lambda),
            w_out_ssm[l].astype(BF16), w_out_lru[l].astype(BF16), w_out[l].astype(BF16))
        h2 = _mixer(h2, batch, seq, weights)
        h2 = _ffn(h2, row2(norm2_w), w_ffn_in[l].astype(BF16), w_ffn_out[l].astype(BF16),
                  norm_f_w.reshape(1, -1), final_norm=(l == depth - 1))
    return h2.reshape(batch, seq, d)
```
